```python
import math
import jax, jax.numpy as jnp
from jax import lax
import numpy as np

D_MODEL = 2048
BATCH = 4
SEQ = 4096
DEPTH = 2

GRID_W = 64
CTX_LEN = 256
EPS = 1e-6
CONV_DIM = 1024
CONV_GROUPS = 16
FOURIER_DIM = 1024
FOURIER_GROUPS = 4
FOURIER_GROUP_DIM = FOURIER_DIM // FOURIER_GROUPS
CONV_WIDTH = 3
EVEN_IN_DIM = 3 * CONV_DIM + FOURIER_DIM
EVEN_OUT_DIM = CONV_DIM + FOURIER_DIM
DIFF_HEADS = 8
DIFF_HEAD_DIM = 128
DIFF_V_DIM = 2 * DIFF_HEAD_DIM
QK_DIM = DIFF_HEADS * 2 * DIFF_HEAD_DIM
V_DIM = DIFF_HEADS * DIFF_V_DIM
ROPE_THETA = 10000.0
ROPE_FREQS = DIFF_HEAD_DIM // 4
Q_BLOCK = 128
LAMBDA_STD = 0.1
N_GROUPS = 4
EXPERTS_PER_GROUP = 4
N_EXPERTS = N_GROUPS * EXPERTS_PER_GROUP
TOP_K = 2
EXPERT_FF = 512
MOE_BLOCK = 128
MOD_SCALE = 0.5

kernel_name = "hybrid_conv_fourier_diffattn_hmoe_dit"


def rmsnorm(x, w):
    xf = x.astype(jnp.float32)
    y = xf * lax.rsqrt(jnp.mean(xf * xf, axis=-1, keepdims=True) + EPS)
    return (y * w.astype(jnp.float32)).astype(x.dtype)


def modulation(cond, w_mod, b_mod):
    m = jax.nn.silu(cond) @ w_mod + b_mod
    return jnp.split(m[..., None, :], 6, axis=-1)


def short_conv3(u, w):
    up = jnp.pad(u, ((0, 0), (1, 1), (0, 0)))
    return up[:, :-2] * w[0] + up[:, 1:-1] * w[1] + up[:, 2:] * w[2]


def conv_fourier_mixer(u, w_in, conv_w, w_out):
    b, n, _ = u.shape
    p = u @ w_in
    b_gate, c_gate, v_in, f_in = jnp.split(p, [CONV_DIM, 2 * CONV_DIM, 3 * CONV_DIM], axis=-1)
    y_conv = b_gate * short_conv3(c_gate * v_in, conv_w)
    fg = f_in.reshape(b, n, FOURIER_GROUPS, FOURIER_GROUP_DIM).astype(jnp.float32)
    y_fourier = jnp.fft.fft2(fg, axes=(1, 3), norm="ortho").real
    y_fourier = y_fourier.reshape(b, n, FOURIER_DIM).astype(u.dtype)
    return jnp.concatenate([y_conv, y_fourier], axis=-1) @ w_out


def axial_rope_tables(n_tok):
    rows = n_tok // GRID_W
    row = jnp.repeat(jnp.arange(rows, dtype=jnp.float32), GRID_W)
    col = jnp.tile(jnp.arange(GRID_W, dtype=jnp.float32), rows)
    inv_freq = ROPE_THETA ** (-jnp.arange(ROPE_FREQS, dtype=jnp.float32) / ROPE_FREQS)
    ang = jnp.stack([row, col], axis=-1)[:, :, None] * inv_freq
    return jnp.cos(ang), jnp.sin(ang)


def apply_axial_rope(x, cos, sin):
    sh = x.shape
    xr = x.reshape(sh[:-1] + (2, 2, ROPE_FREQS)).astype(jnp.float32)
    x1, x2 = xr[..., 0, :], xr[..., 1, :]
    cc = cos[None, :, None, None]
    ss = sin[None, :, None, None]
    out = jnp.stack([x1 * cc - x2 * ss, x2 * cc + x1 * ss], axis=-2)
    return out.reshape(sh).astype(x.dtype)


def diff_attend(q, k, v, lam):
    s = jnp.einsum('bqhcd,bkhcd->bhcqk', q, k).astype(jnp.float32) * (DIFF_HEAD_DIM ** -0.5)
    p = jax.nn.softmax(s, axis=-1)
    a = p[:, :, 0] - lam * p[:, :, 1]
    return jnp.einsum('bhqk,bkhe->bqhe', a.astype(v.dtype), v)


def blockwise_diff_attend(q, k, v, lam):
    b, nq = q.shape[:2]
    nb = nq // Q_BLOCK
    qb = jnp.moveaxis(q.reshape((b, nb, Q_BLOCK) + q.shape[2:]), 1, 0)
    ob = lax.map(lambda qq: diff_attend(qq, k, v, lam), qb)
    return jnp.moveaxis(ob, 0, 1).reshape((b, nq) + ob.shape[3:])


def diff_attention_mixer(u_lat, u_ctx, layer_idx, cos, sin, w_qkv, q_norm, k_norm,
                         lam_q1, lam_k1, lam_q2, lam_k2, subln_w, w_out, need_ctx):
    f32 = jnp.float32
    lam_init = 0.8 - 0.6 * math.exp(-0.3 * layer_idx)
    lam = (jnp.exp(jnp.sum(lam_q1.astype(f32) * lam_k1.astype(f32)))
           - jnp.exp(jnp.sum(lam_q2.astype(f32) * lam_k2.astype(f32))) + lam_init)
    w_q, w_kv = w_qkv[:, :QK_DIM], w_qkv[:, QK_DIM:]

    def split_qk(t, gain):
        b, n, _ = t.shape
        return rmsnorm(t.reshape(b, n, DIFF_HEADS, 2, DIFF_HEAD_DIM), gain)

    def keys_values(u):
        b, n, _ = u.shape
        kv = u @ w_kv
        return split_qk(kv[..., :QK_DIM], k_norm), kv[..., QK_DIM:].reshape(b, n, DIFF_HEADS, DIFF_V_DIM)

    def merge(o):
        b, n = o.shape[:2]
        o = rmsnorm(o, subln_w) * (1.0 - lam_init)
        return o.reshape(b, n, V_DIM) @ w_out

    k_ctx, v_ctx = keys_values(u_ctx)
    k_lat, v_lat = keys_values(u_lat)
    k_lat = apply_axial_rope(k_lat, cos, sin)
    q_lat = apply_axial_rope(split_qk(u_lat @ w_q, q_norm), cos, sin)
    k_all = jnp.concatenate([k_ctx, k_lat], axis=1)
    v_all = jnp.concatenate([v_ctx, v_lat], axis=1)
    out_lat = merge(blockwise_diff_attend(q_lat, k_all, v_all, lam))
    if need_ctx:
        q_ctx = split_qk(u_ctx @ w_q, q_norm)
        out_ctx = merge(diff_attend(q_ctx, k_ctx, v_ctx, lam))
    else:
        out_ctx = None
    return out_lat, out_ctx


def hier_moe(h, w_rg, b_rg, w_re, b_re, w_gate, w_up, w_down):
    f32 = jnp.float32
    n, d = h.shape
    g_logits = (h @ w_rg).astype(f32) + b_rg.astype(f32)
    g_idx = jnp.argmax(g_logits, axis=-1)
    g_w = jnp.max(jax.nn.softmax(g_logits, axis=-1), axis=-1)
    e_logits = ((h @ w_re).astype(f32) + b_re.astype(f32)).reshape(n, N_GROUPS, EXPERTS_PER_GROUP)
    e_sel = e_logits[jnp.arange(n), g_idx]
    top_v, top_i = lax.top_k(e_sel, TOP_K)
    top_w = jax.nn.softmax(top_v, axis=-1) * g_w[:, None]
    expert = (g_idx[:, None] * EXPERTS_PER_GROUP + top_i).reshape(-1).astype(jnp.int32)
    token = jnp.repeat(jnp.arange(n, dtype=jnp.int32), TOP_K)
    weight = top_w.reshape(-1)
    n_assign = n * TOP_K
    n_blocks = -(-n_assign // MOE_BLOCK) + N_EXPERTS
    n_rows = n_blocks * MOE_BLOCK
    counts = jnp.zeros((N_EXPERTS,), jnp.int32).at[expert].add(1)
    blocks_per = (counts + MOE_BLOCK - 1) // MOE_BLOCK
    block_end = jnp.cumsum(blocks_per)
    row_start = (block_end - blocks_per) * MOE_BLOCK
    assign_start = jnp.cumsum(counts) - counts
    order = jnp.argsort(expert)
    e_sorted = expert[order]
    dest = row_start[e_sorted] + (jnp.arange(n_assign, dtype=jnp.int32) - assign_start[e_sorted])
    row_token = jnp.zeros((n_rows,), jnp.int32).at[dest].set(token[order])
    row_weight = jnp.zeros((n_rows,), f32).at[dest].set(weight[order])
    block_expert = jnp.minimum(jnp.searchsorted(block_end, jnp.arange(n_blocks, dtype=jnp.int32), side='right'),
                               N_EXPERTS - 1)

    def run_block(args):
        tok_b, w_b, e_b = args
        xb = h[tok_b]
        yb = (jax.nn.silu(xb @ w_gate[e_b]) * (xb @ w_up[e_b])) @ w_down[e_b]
        return yb * w_b[:, None].astype(yb.dtype)

    y_rows = lax.map(run_block, (row_token.reshape(n_blocks, MOE_BLOCK),
                                 row_weight.reshape(n_blocks, MOE_BLOCK), block_expert))
    return jnp.zeros_like(h).at[row_token].add(y_rows.reshape(n_rows, d))


def setup_inputs(seed: int = 0) -> dict:
    key = jax.random.key(seed)
    ks = jax.random.split(key, 32)
    D = D_MODEL
    n_even = (DEPTH + 1) // 2
    n_odd = DEPTH // 2
    f32 = jnp.float32

    def nrm(k, shape, std):
        return jax.random.normal(k, shape, f32) * std

    return {
        "x": nrm(ks[0], (BATCH, SEQ, D), 1.0),
        "c": nrm(ks[1], (BATCH, D), 1.0),
        "ctx": nrm(ks[2], (BATCH, CTX_LEN, D), 1.0),
        "c_ctx": nrm(ks[3], (D,), 1.0),
        "norm1_w": 1.0 + nrm(ks[4], (DEPTH, D), 0.02),
        "norm2_w": 1.0 + nrm(ks[5], (DEPTH, D), 0.02),
        "w_mod": nrm(ks[6], (DEPTH, D, 6 * D), MOD_SCALE * D ** -0.5),
        "b_mod": nrm(ks[7], (DEPTH, 6 * D), 0.02),
        "even_w_in": nrm(ks[8], (n_even, D, EVEN_IN_DIM), D ** -0.5),
        "even_conv_w": nrm(ks[9], (n_even, CONV_WIDTH, CONV_DIM), CONV_WIDTH ** -0.5),
        "even_w_out": nrm(ks[10], (n_even, EVEN_OUT_DIM, D), EVEN_OUT_DIM ** -0.5),
        "odd_w_qkv": nrm(ks[11], (n_odd, D, 2 * QK_DIM + V_DIM), D ** -0.5),
        "odd_q_norm": 1.0 + nrm(ks[12], (n_odd, DIFF_HEAD_DIM), 0.02),
        "odd_k_norm": 1.0 + nrm(ks[13], (n_odd, DIFF_HEAD_DIM), 0.02),
        "odd_lambda_q1": nrm(ks[14], (n_odd, DIFF_HEAD_DIM), LAMBDA_STD),
        "odd_lambda_k1": nrm(ks[15], (n_odd, DIFF_HEAD_DIM), LAMBDA_STD),
        "odd_lambda_q2": nrm(ks[16], (n_odd, DIFF_HEAD_DIM), LAMBDA_STD),
        "odd_lambda_k2": nrm(ks[17], (n_odd, DIFF_HEAD_DIM), LAMBDA_STD),
        "odd_subln_w": 1.0 + nrm(ks[18], (n_odd, DIFF_V_DIM), 0.02),
        "odd_w_out": nrm(ks[19], (n_odd, V_DIM, D), V_DIM ** -0.5),
        "moe_w_group": nrm(ks[20], (DEPTH, D, N_GROUPS), D ** -0.5),
        "moe_b_group": nrm(ks[21], (DEPTH, N_GROUPS), 0.01),
        "moe_w_expert": nrm(ks[22], (DEPTH, D, N_EXPERTS), D ** -0.5),
        "moe_b_expert": nrm(ks[23], (DEPTH, N_EXPERTS), 0.01),
        "moe_w_gate": nrm(ks[24], (DEPTH, N_EXPERTS, D, EXPERT_FF), D ** -0.5),
        "moe_w_up": nrm(ks[25], (DEPTH, N_EXPERTS, D, EXPERT_FF), D ** -0.5),
        "moe_w_down": nrm(ks[26], (DEPTH, N_EXPERTS, EXPERT_FF, D), EXPERT_FF ** -0.5),
    }


def reference(x, c, ctx, c_ctx, norm1_w, norm2_w, w_mod, b_mod,
              even_w_in, even_conv_w, even_w_out,
              odd_w_qkv, odd_q_norm, odd_k_norm, odd_lambda_q1, odd_lambda_k1,
              odd_lambda_q2, odd_lambda_k2, odd_subln_w, odd_w_out,
              moe_w_group, moe_b_group, moe_w_expert, moe_b_expert,
              moe_w_gate, moe_w_up, moe_w_down):
    bsz, n_lat, d = x.shape
    n_ctx = ctx.shape[1]
    cos, sin = axial_rope_tables(n_lat)
    h_lat, h_ctx = x, ctx
    for l in range(DEPTH):
        last = l == DEPTH - 1
        i = l // 2
        sh1, sc1, g1, sh2, sc2, g2 = modulation(c, w_mod[l], b_mod[l])
        csh1, csc1, cg1, csh2, csc2, cg2 = modulation(c_ctx, w_mod[l], b_mod[l])
        u_lat = rmsnorm(h_lat, norm1_w[l]) * (1.0 + sc1) + sh1
        u_ctx = rmsnorm(h_ctx, norm1_w[l]) * (1.0 + csc1) + csh1
        if l % 2 == 0:
            m_lat = conv_fourier_mixer(u_lat, even_w_in[i], even_conv_w[i], even_w_out[i])
            m_ctx = None if last else conv_fourier_mixer(u_ctx, even_w_in[i], even_conv_w[i], even_w_out[i])
        else:
            m_lat, m_ctx = diff_attention_mixer(
                u_lat, u_ctx, l, cos, sin, odd_w_qkv[i], odd_q_norm[i], odd_k_norm[i],
                odd_lambda_q1[i], odd_lambda_k1[i], odd_lambda_q2[i], odd_lambda_k2[i],
                odd_subln_w[i], odd_w_out[i], not last)
        h_lat = h_lat + g1 * m_lat
        v_lat = rmsnorm(h_lat, norm2_w[l]) * (1.0 + sc2) + sh2
        moe_args = (moe_w_group[l], moe_b_group[l], moe_w_expert[l], moe_b_expert[l],
                    moe_w_gate[l], moe_w_up[l], moe_w_down[l])
        if last:
            y_lat = hier_moe(v_lat.reshape(-1, d), *moe_args)
            h_lat = h_lat + g2 * y_lat.reshape(bsz, n_lat, d)
        else:
            h_ctx = h_ctx + cg1 * m_ctx
            v_ctx = rmsnorm(h_ctx, norm2_w[l]) * (1.0 + csc2) + csh2
            tokens = jnp.concatenate([v_lat.reshape(-1, d), v_ctx.reshape(-1, d)], axis=0)
            y = hier_moe(tokens, *moe_args)
            y_lat = y[:bsz * n_lat].reshape(bsz, n_lat, d)
            y_ctx = y[bsz * n_lat:].reshape(bsz, n_ctx, d)
            h_lat = h_lat + g2 * y_lat
            h_ctx = h_ctx + cg2 * y_ctx
    return h_lat
```

```python
import functools
import math

import jax
import jax.numpy as jnp
import numpy as np
from jax import lax
from jax.experimental import pallas as pl
from jax.experimental.pallas import tpu as pltpu

F32 = jnp.float32
BF16 = jnp.bfloat16

GRID_W = 64
EPS = 1e-6
CONV_DIM = 1024
FOURIER_DIM = 1024
FOURIER_GROUPS = 4
FOURIER_GROUP_DIM = FOURIER_DIM // FOURIER_GROUPS
DIFF_HEADS = 8
DIFF_HEAD_DIM = 128
DIFF_V_DIM = 2 * DIFF_HEAD_DIM
QK_DIM = DIFF_HEADS * 2 * DIFF_HEAD_DIM
V_DIM = DIFF_HEADS * DIFF_V_DIM
ROPE_THETA = 10000.0
ROPE_FREQS = DIFF_HEAD_DIM // 4
N_GROUPS = 4
EXPERTS_PER_GROUP = 4
N_EXPERTS = N_GROUPS * EXPERTS_PER_GROUP
EXPERT_FF = 512

LANES = 128
BF16_SUBLANES = 16
MOD_ROWS = 8
VMEM_LIMIT_BYTES = 56 * 1024 * 1024

_PAIRS = ((0, 1), (0, 2), (0, 3), (1, 2), (1, 3), (2, 3))
N_CLASSES = N_GROUPS * len(_PAIRS)
_CLASS_LO = np.array([EXPERTS_PER_GROUP * g + a for g in range(N_GROUPS) for a, _ in _PAIRS], np.int32)
_CLASS_HI = np.array([EXPERTS_PER_GROUP * g + b for g in range(N_GROUPS) for _, b in _PAIRS], np.int32)
ROUTER_COLS = LANES
ROUTER_HALF = 32
MOE_BLOCK = 256
GATHER_WINDOW = 32


def _params(*semantics):
    return pltpu.CompilerParams(dimension_semantics=semantics, vmem_limit_bytes=VMEM_LIMIT_BYTES)


def _mod_kernel(c_ref, w_ref, b_ref, o_ref):
    c = c_ref[...]
    s = c * jax.nn.sigmoid(c)
    hi = s.astype(BF16)
    lo = (s - hi.astype(F32)).astype(BF16)
    lhs = jnp.concatenate([hi, lo], axis=0)
    acc = jnp.dot(lhs, w_ref[0].astype(BF16), preferred_element_type=F32)
    o_ref[0] = acc[:MOD_ROWS] + acc[MOD_ROWS:] + b_ref[0]


def _modulation(cond, w_mod, b_mod, tn=1024):
    depth, d, n = w_mod.shape
    return pl.pallas_call(
        _mod_kernel,
        grid=(depth, n // tn),
        in_specs=[
            pl.BlockSpec((MOD_ROWS, d), lambda l, j: (0, 0)),
            pl.BlockSpec((1, d, tn), lambda l, j: (l, 0, j)),
            pl.BlockSpec((1, 1, tn), lambda l, j: (l, 0, j)),
        ],
        out_specs=pl.BlockSpec((1, MOD_ROWS, tn), lambda l, j: (l, 0, j)),
        out_shape=jax.ShapeDtypeStruct((depth, MOD_ROWS, n), F32),
        compiler_params=_params("arbitrary", "arbitrary"),
        name="modulation",
    )(cond, w_mod, b_mod.reshape(depth, 1, n))


def _ln_mod(x, scale, shift):
    r = lax.rsqrt(jnp.mean(x * x, axis=-1, keepdims=True) + EPS)
    return x * r * scale + shift


def _mod_row_map(tm, n_lat, n_batch):
    return lambda i, j: (jnp.minimum((i * tm) // n_lat, n_batch), 0, 0)


def _ln_matmul_kernel(h_ref, s_ref, sh_ref, w_ref, o_ref, u_ref):
    @pl.when(pl.program_id(1) == 0)
    def _():
        u_ref[...] = _ln_mod(h_ref[...], s_ref[0], sh_ref[0]).astype(BF16)

    o_ref[...] = jnp.dot(u_ref[...], w_ref[...], preferred_element_type=F32).astype(o_ref.dtype)


def _ln_matmul(h, scale, shift, w, n_rows, tm, tn, n_lat, n_batch):
    d = h.shape[1]
    n = w.shape[1]
    mod_map = _mod_row_map(tm, n_lat, n_batch)
    return pl.pallas_call(
        _ln_matmul_kernel,
        grid=(n_rows // tm, n // tn),
        in_specs=[
            pl.BlockSpec((tm, d), lambda i, j: (i, 0)),
            pl.BlockSpec((1, 1, d), mod_map),
            pl.BlockSpec((1, 1, d), mod_map),
            pl.BlockSpec((d, tn), lambda i, j: (0, j)),
        ],
        out_specs=pl.BlockSpec((tm, tn), lambda i, j: (i, j)),
        out_shape=jax.ShapeDtypeStruct((n_rows, n), BF16),
        scratch_shapes=[pltpu.VMEM((tm, d), BF16)],
        compiler_params=_params("arbitrary", "arbitrary"),
        name="ln_matmul",
    )(h, scale, shift, w)


def _qkv_kernel(h_ref, s_ref, sh_ref, w_ref, tab_ref, o_ref, u_ref, *, n_qk_tiles):
    j = pl.program_id(1)

    @pl.when(j == 0)
    def _():
        u_ref[...] = _ln_mod(h_ref[...], s_ref[0], sh_ref[0]).astype(BF16)

    acc = jnp.dot(u_ref[...], w_ref[...], preferred_element_type=F32)

    @pl.when(j < n_qk_tiles)
    def _():
        cos = tab_ref[0, :, 0:LANES]
        sin_a = tab_ref[0, :, LANES:2 * LANES]
        sin_b = tab_ref[0, :, 2 * LANES:3 * LANES]
        for c in range(acc.shape[1] // DIFF_HEAD_DIM):
            x = acc[:, c * DIFF_HEAD_DIM:(c + 1) * DIFF_HEAD_DIM]
            r = lax.rsqrt(jnp.mean(x * x, axis=-1, keepdims=True) + EPS)
            fwd = pltpu.roll(x, DIFF_HEAD_DIM - ROPE_FREQS, 1)
            bwd = pltpu.roll(x, ROPE_FREQS, 1)
            y = (x * cos + fwd * sin_a + bwd * sin_b) * r
            o_ref[:, c * DIFF_HEAD_DIM:(c + 1) * DIFF_HEAD_DIM] = y.astype(o_ref.dtype)

    @pl.when(j >= n_qk_tiles)
    def _():
        o_ref[...] = acc.astype(o_ref.dtype)


def _rope_tables(n_lat, tm, q_gain, k_gain):
    rows = n_lat // GRID_W
    row = jnp.repeat(jnp.arange(rows, dtype=F32), GRID_W)
    col = jnp.tile(jnp.arange(GRID_W, dtype=F32), rows)
    inv_freq = ROPE_THETA ** (-jnp.arange(ROPE_FREQS, dtype=F32) / ROPE_FREQS)
    ang = jnp.stack([row, col], axis=-1)[:, :, None] * inv_freq
    cos, sin = jnp.cos(ang), jnp.sin(ang)
    zero = jnp.zeros_like(sin)
    cos_l = jnp.stack([cos, cos], axis=2).reshape(n_lat, DIFF_HEAD_DIM)
    sin_a = jnp.stack([-sin, zero], axis=2).reshape(n_lat, DIFF_HEAD_DIM)
    sin_b = jnp.stack([zero, sin], axis=2).reshape(n_lat, DIFF_HEAD_DIM)
    pad = lambda a, v: jnp.concatenate([a, jnp.full((tm, DIFF_HEAD_DIM), v, F32)], axis=0)
    cos_l, sin_a, sin_b = pad(cos_l, 1.0), pad(sin_a, 0.0), pad(sin_b, 0.0)

    def fold(gain):
        g = gain.astype(F32)
        g_fwd = jnp.roll(g, -ROPE_FREQS)
        g_bwd = jnp.roll(g, ROPE_FREQS)
        return jnp.concatenate([cos_l * g, sin_a * g_fwd, sin_b * g_bwd], axis=1)

    return jnp.stack([fold(q_gain), fold(k_gain)], axis=0)


def _qkv_proj(h, scale, shift, w, tab, n_rows, tm, tn, n_lat, n_batch):
    d = h.shape[1]
    n = w.shape[1]
    n_qk_tiles = 2 * QK_DIM // tn
    lat_tiles = n_lat // tm
    n_lat_tiles = n_batch * lat_tiles
    mod_map = _mod_row_map(tm, n_lat, n_batch)

    def tab_map(i, j):
        which = jnp.minimum(j // (QK_DIM // tn), 1)
        return (which, jnp.where(i < n_lat_tiles, i % lat_tiles, lat_tiles), 0)

    return pl.pallas_call(
        functools.partial(_qkv_kernel, n_qk_tiles=n_qk_tiles),
        grid=(n_rows // tm, n // tn),
        in_specs=[
            pl.BlockSpec((tm, d), lambda i, j: (i, 0)),
            pl.BlockSpec((1, 1, d), mod_map),
            pl.BlockSpec((1, 1, d), mod_map),
            pl.BlockSpec((d, tn), lambda i, j: (0, j)),
            pl.BlockSpec((1, tm, 3 * LANES), tab_map),
        ],
        out_specs=pl.BlockSpec((tm, tn), lambda i, j: (i, j)),
        out_shape=jax.ShapeDtypeStruct((n_rows, n), BF16),
        scratch_shapes=[pltpu.VMEM((tm, d), BF16)],
        compiler_params=_params("arbitrary", "arbitrary"),
        name="qkv_proj",
    )(h, scale, shift, w, tab)


def _proj_res_kernel(xa_ref, xb_ref, wa_ref, wb_ref, h_ref, g_ref, o_ref):
    acc = jnp.dot(xa_ref[...], wa_ref[...], preferred_element_type=F32)
    acc += jnp.dot(xb_ref[...], wb_ref[...], preferred_element_type=F32)
    o_ref[...] = h_ref[...] + g_ref[0] * acc


def _proj_residual(xa, xb, col_a, col_b, w, h, gate, n_rows, tm, tn, n_lat, n_batch):
    k, n = w.shape
    kh = k // 2
    mod_map3 = _mod_row_map(tm, n_lat, n_batch)
    return pl.pallas_call(
        _proj_res_kernel,
        grid=(n_rows // tm, n // tn),
        in_specs=[
            pl.BlockSpec((tm, kh), lambda i, j: (i, col_a)),
            pl.BlockSpec((tm, kh), lambda i, j: (i, col_b)),
            pl.BlockSpec((kh, tn), lambda i, j: (0, j)),
            pl.BlockSpec((kh, tn), lambda i, j: (1, j)),
            pl.BlockSpec((tm, tn), lambda i, j: (i, j)),
            pl.BlockSpec((1, 1, tn), lambda i, j: mod_map3(i, j)[:1] + (0, j)),
        ],
        out_specs=pl.BlockSpec((tm, tn), lambda i, j: (i, j)),
        out_shape=jax.ShapeDtypeStruct((n_rows, n), F32),
        compiler_params=_params("arbitrary", "arbitrary"),
        name="proj_residual",
    )(xa, xb, w, w, h, gate)


def _conv_kernel(b_ref, c_ref, v_ref, cp_ref, vp_ref, cn_ref, vn_ref, w_ref, o_ref, *,
                 tm, n_lat, n_lat_rows, n_ctx):
    row0 = pl.program_id(0) * tm
    seq_len = jnp.where(row0 < n_lat_rows, n_lat, n_ctx)
    has_prev = (row0 % seq_len) != 0
    has_next = ((row0 + tm) % seq_len) != 0
    cv = c_ref[...].astype(F32) * v_ref[...].astype(F32)
    last = BF16_SUBLANES - 1
    prev = cp_ref[last:last + 1, :].astype(F32) * vp_ref[last:last + 1, :].astype(F32)
    nxt = cn_ref[0:1, :].astype(F32) * vn_ref[0:1, :].astype(F32)
    prev = jnp.where(has_prev, prev, 0.0)
    nxt = jnp.where(has_next, nxt, 0.0)
    rows = lax.broadcasted_iota(jnp.int32, cv.shape, 0)
    up = jnp.where(rows == 0, prev, pltpu.roll(cv, 1, 0))
    dn = jnp.where(rows == tm - 1, nxt, pltpu.roll(cv, tm - 1, 0))
    w = w_ref[...]
    y = b_ref[...].astype(F32) * (up * w[0:1] + cv * w[1:2] + dn * w[2:3])
    o_ref[...] = y.astype(o_ref.dtype)


def _gated_conv(p, conv_w, n_rows, tm, n_lat, n_batch, n_ctx):
    halo = BF16_SUBLANES
    hb = tm // halo
    n_halo_blocks = n_rows // halo
    w8 = jnp.zeros((MOD_ROWS, CONV_DIM), F32).at[:conv_w.shape[0]].set(conv_w.astype(F32))
    main = lambda col: pl.BlockSpec((tm, CONV_DIM), lambda i: (i, col))
    prev = lambda col: pl.BlockSpec((halo, CONV_DIM), lambda i: (jnp.maximum(i * hb - 1, 0), col))
    nxt = lambda col: pl.BlockSpec((halo, CONV_DIM), lambda i: (jnp.minimum((i + 1) * hb, n_halo_blocks - 1), col))
    return pl.pallas_call(
        functools.partial(_conv_kernel, tm=tm, n_lat=n_lat, n_lat_rows=n_batch * n_lat, n_ctx=n_ctx),
        grid=(n_rows // tm,),
        in_specs=[main(0), main(1), main(2), prev(1), prev(2), nxt(1), nxt(2),
                  pl.BlockSpec((MOD_ROWS, CONV_DIM), lambda i: (0, 0))],
        out_specs=pl.BlockSpec((tm, CONV_DIM), lambda i: (i, 0)),
        out_shape=jax.ShapeDtypeStruct((n_rows, CONV_DIM), BF16),
        compiler_params=_params("arbitrary"),
        name="gated_conv",
    )(p, p, p, p, p, p, p, w8)


def _dft_tables(n):
    k = jnp.arange(n, dtype=jnp.int32)
    ang = ((k[:, None] * k[None, :]) % n).astype(F32) * (2.0 * math.pi / n)
    s = n ** -0.5
    return jnp.cos(ang) * s, jnp.sin(ang) * s


def _seq_dft_tables(n):
    if n % 64 or n <= 64:
        c, s = _dft_tables(n)
        return c.astype(BF16), (-s).astype(BF16)
    m = n // 64
    j1 = jnp.arange(64, dtype=jnp.int32)
    j2 = jnp.arange(m, dtype=jnp.int32)
    k = jnp.arange(n, dtype=jnp.int32)
    a = ((j1[:, None] * k[None, :]) % 64).astype(F32) * (2.0 * math.pi / 64)
    b = ((j2[:, None] * k[None, :]) % n).astype(F32) * (2.0 * math.pi / n)
    ca, sa, cb, sb = jnp.cos(a)[:, None], jnp.sin(a)[:, None], jnp.cos(b)[None], jnp.sin(b)[None]
    scale = n ** -0.5
    c = ((ca * cb - sa * sb) * scale).reshape(n, n)
    s = ((sa * cb + ca * sb) * scale).reshape(n, n)
    return c.astype(BF16), (-s).astype(BF16)


def _chan_dft_kernel(x_ref, cs_ref, a_ref, b_ref):
    gd = FOURIER_GROUP_DIM
    for g in range(FOURIER_GROUPS):
        ab = jnp.dot(x_ref[:, g * gd:(g + 1) * gd], cs_ref[...], preferred_element_type=F32)
        a_ref[:, g * gd:(g + 1) * gd] = ab[:, :gd].astype(a_ref.dtype)
        b_ref[:, g * gd:(g + 1) * gd] = ab[:, gd:].astype(b_ref.dtype)


def _chan_dft(p, n_rows, tm):
    c, s = _dft_tables(FOURIER_GROUP_DIM)
    cs = jnp.concatenate([c, s], axis=1).astype(BF16)
    out = jax.ShapeDtypeStruct((n_rows, FOURIER_DIM), BF16)
    return pl.pallas_call(
        _chan_dft_kernel,
        grid=(n_rows // tm,),
        in_specs=[pl.BlockSpec((tm, FOURIER_DIM), lambda i: (i, 3)),
                  pl.BlockSpec((FOURIER_GROUP_DIM, 2 * FOURIER_GROUP_DIM), lambda i: (0, 0))],
        out_specs=[pl.BlockSpec((tm, FOURIER_DIM), lambda i: (i, 0))] * 2,
        out_shape=[out, out],
        compiler_params=_params("arbitrary"),
        name="chan_dft",
    )(p, cs)


def _seq_dft_kernel(cn_ref, sn_ref, a_ref, b_ref, o_ref):
    acc = jnp.dot(cn_ref[...], a_ref[...], preferred_element_type=F32)
    acc += jnp.dot(sn_ref[...], b_ref[...], preferred_element_type=F32)
    o_ref[...] = acc.astype(o_ref.dtype)


def _seq_dft_lat(a, b, n_lat, n_batch, tm, tn):
    cn, sn = _seq_dft_tables(n_lat)
    n_rows = n_batch * n_lat
    mt = n_lat // tm
    return pl.pallas_call(
        _seq_dft_kernel,
        grid=(n_batch, FOURIER_DIM // tn, mt),
        in_specs=[pl.BlockSpec((tm, n_lat), lambda bi, j, i: (i, 0)),
                  pl.BlockSpec((tm, n_lat), lambda bi, j, i: (i, 0)),
                  pl.BlockSpec((n_lat, tn), lambda bi, j, i: (bi, j)),
                  pl.BlockSpec((n_lat, tn), lambda bi, j, i: (bi, j))],
        out_specs=pl.BlockSpec((tm, tn), lambda bi, j, i: (bi * mt + i, j)),
        out_shape=jax.ShapeDtypeStruct((n_rows, FOURIER_DIM), BF16),
        compiler_params=_params("arbitrary", "arbitrary", "arbitrary"),
        name="seq_dft_lat",
    )(cn, sn, a, b)


def _seq_dft_ctx(a, b, n_lat, n_batch, n_ctx):
    cn, sn = _seq_dft_tables(n_ctx)
    first = n_batch * n_lat // n_ctx
    blk = lambda: pl.BlockSpec((n_ctx, FOURIER_DIM), lambda bi: (first + bi, 0))
    tab = lambda: pl.BlockSpec((n_ctx, n_ctx), lambda bi: (0, 0))
    return pl.pallas_call(
        _seq_dft_kernel,
        grid=(n_batch,),
        in_specs=[tab(), tab(), blk(), blk()],
        out_specs=pl.BlockSpec((n_ctx, FOURIER_DIM), lambda bi: (bi, 0)),
        out_shape=jax.ShapeDtypeStruct((n_batch * n_ctx, FOURIER_DIM), BF16),
        compiler_params=_params("arbitrary"),
        name="seq_dft_ctx",
    )(cn, sn, a, b)


def _attn_kernel(lam_ref, q_ref, kl_ref, kc_ref, vl_ref, vc_ref, g_ref, o_ref, s1_ref, s2_ref, *,
                 n_lat, n_ctx, ck):
    dh = DIFF_HEAD_DIM
    nt = (((1,), (1,)), ((), ()))
    q = q_ref[...]
    q1, q2 = q[:, :dh], q[:, dh:]
    chunks = [(kl_ref, c * ck, ck, c * ck) for c in range(n_lat // ck)] + [(kc_ref, 0, n_ctx, n_lat)]
    for ref, src, size, dst in chunks:
        k = ref[src:src + size, :]
        s1_ref[:, dst:dst + size] = lax.dot_general(q1, k[:, :dh], nt, preferred_element_type=F32)
        s2_ref[:, dst:dst + size] = lax.dot_general(q2, k[:, dh:], nt, preferred_element_type=F32)
    s1 = s1_ref[...]
    s2 = s2_ref[...]
    p1 = jnp.exp(s1 - jnp.max(s1, axis=-1, keepdims=True))
    p2 = jnp.exp(s2 - jnp.max(s2, axis=-1, keepdims=True))
    r1 = 1.0 / jnp.sum(p1, axis=-1, keepdims=True)
    r2 = lam_ref[0] / jnp.sum(p2, axis=-1, keepdims=True)
    a = (p1 * r1 - p2 * r2).astype(BF16)
    o = jnp.dot(a[:, :n_lat], vl_ref[...], preferred_element_type=F32)
    o += jnp.dot(a[:, n_lat:], vc_ref[...], preferred_element_type=F32)
    r = lax.rsqrt(jnp.mean(o * o, axis=-1, keepdims=True) + EPS)
    o_ref[...] = (o * r * g_ref[...]).astype(o_ref.dtype)


def _diff_attention(qkv, lam, subln_gain, n_lat, n_batch, n_ctx, tq, ck):
    hd = DIFF_V_DIM
    qt = n_lat // tq
    k0 = QK_DIM // hd
    v0 = 2 * QK_DIM // hd
    ctx0 = n_batch * n_lat // n_ctx
    return pl.pallas_call(
        functools.partial(_attn_kernel, n_lat=n_lat, n_ctx=n_ctx, ck=ck),
        grid=(n_batch, DIFF_HEADS, qt),
        in_specs=[
            pl.BlockSpec(memory_space=pltpu.SMEM),
            pl.BlockSpec((tq, hd), lambda b, h, i: (b * qt + i, h)),
            pl.BlockSpec((n_lat, hd), lambda b, h, i: (b, k0 + h)),
            pl.BlockSpec((n_ctx, hd), lambda b, h, i: (ctx0 + b, k0 + h)),
            pl.BlockSpec((n_lat, hd), lambda b, h, i: (b, v0 + h)),
            pl.BlockSpec((n_ctx, hd), lambda b, h, i: (ctx0 + b, v0 + h)),
            pl.BlockSpec((1, hd), lambda b, h, i: (0, 0)),
        ],
        out_specs=pl.BlockSpec((tq, hd), lambda b, h, i: (b * qt + i, h)),
        out_shape=jax.ShapeDtypeStruct((n_batch * n_lat, V_DIM), BF16),
        scratch_shapes=[pltpu.VMEM((tq, n_lat + n_ctx), F32), pltpu.VMEM((tq, n_lat + n_ctx), F32)],
        compiler_params=_params("arbitrary", "arbitrary", "arbitrary"),
        name="diff_attention",
    )(lam, qkv, qkv, qkv, qkv, qkv, subln_gain)


def _router_kernel(h_ref, s_ref, sh_ref, w_ref, b_ref, v_ref, cls_ref, wt_ref):
    v = _ln_mod(h_ref[...], s_ref[0], sh_ref[0])
    v_hi = v.astype(BF16)
    v_lo = (v - v_hi.astype(F32)).astype(BF16)
    v_ref[...] = v_hi
    w = w_ref[...]
    lg = jnp.dot(v_hi, w, preferred_element_type=F32) + jnp.dot(v_lo, w, preferred_element_type=F32)
    lgt = lg.T
    lt = lgt[:ROUTER_HALF] + lgt[ROUTER_HALF:2 * ROUTER_HALF] + b_ref[...]

    def row(i):
        return lt[i:i + 1, :]

    best, g_idx = row(0), jnp.zeros_like(row(0), dtype=jnp.int32)
    for g in range(1, N_GROUPS):
        upd = row(g) > best
        g_idx = jnp.where(upd, g, g_idx)
        best = jnp.where(upd, row(g), best)
    denom = sum(jnp.exp(row(g) - best) for g in range(N_GROUPS))
    g_w = 1.0 / denom

    e = []
    for j in range(EXPERTS_PER_GROUP):
        sel = row(N_GROUPS + j)
        for g in range(1, N_GROUPS):
            sel = jnp.where(g_idx == g, row(N_GROUPS + EXPERTS_PER_GROUP * g + j), sel)
        e.append(sel)
    v1, i1 = e[0], jnp.zeros_like(g_idx)
    for j in range(1, EXPERTS_PER_GROUP):
        upd = e[j] > v1
        i1 = jnp.where(upd, j, i1)
        v1 = jnp.where(upd, e[j], v1)
    neg = jnp.float32(-jnp.inf)
    v2, i2 = jnp.full_like(v1, neg), jnp.full_like(i1, -1)
    for j in range(EXPERTS_PER_GROUP):
        upd = (i1 != j) & ((e[j] > v2) | (i2 < 0))
        i2 = jnp.where(upd, j, i2)
        v2 = jnp.where(upd, e[j], v2)
    t = jnp.exp(v2 - v1)
    w1 = g_w / (1.0 + t)
    w2 = g_w * t / (1.0 + t)
    first_lo = i1 < i2
    lo = jnp.where(first_lo, i1, i2)
    hi = jnp.where(first_lo, i2, i1)
    pair = jnp.where(lo == 0, hi - 1, jnp.where(lo == 1, hi + 1, 5))
    cls_ref[...] = jnp.broadcast_to(len(_PAIRS) * g_idx + pair, cls_ref.shape)
    wt_ref[...] = jnp.concatenate(
        [jnp.where(first_lo, w1, w2), jnp.where(first_lo, w2, w1), jnp.zeros((MOD_ROWS - 2, w1.shape[1]), F32)], axis=0)


def _router(h, scale, shift, w_group, b_group, w_expert, b_expert, n_rows, tm, n_lat, n_batch):
    d = h.shape[1]
    n_logits = N_GROUPS + N_EXPERTS
    half = ROUTER_HALF
    w = jnp.concatenate([w_group, w_expert], axis=1).astype(F32)
    w_hi = w.astype(BF16)
    w_lo = (w - w_hi.astype(F32)).astype(BF16)
    pad = lambda a: jnp.pad(a, ((0, 0), (0, half - n_logits)))
    w_cat = jnp.concatenate([pad(w_hi), pad(w_lo), jnp.zeros((d, ROUTER_COLS - 2 * half), BF16)], axis=1)
    bias = jnp.concatenate([b_group, b_expert]).astype(F32)
    bias = jnp.pad(bias, (0, half - n_logits)).reshape(half, 1)
    mod_map = lambda i: _mod_row_map(tm, n_lat, n_batch)(i, 0)
    return pl.pallas_call(
        _router_kernel,
        grid=(n_rows // tm,),
        in_specs=[
            pl.BlockSpec((tm, d), lambda i: (i, 0)),
            pl.BlockSpec((1, 1, d), mod_map),
            pl.BlockSpec((1, 1, d), mod_map),
            pl.BlockSpec((d, ROUTER_COLS), lambda i: (0, 0)),
            pl.BlockSpec((ROUTER_HALF, 1), lambda i: (0, 0)),
        ],
        out_specs=[
            pl.BlockSpec((tm, d), lambda i: (i, 0)),
            pl.BlockSpec((MOD_ROWS, tm), lambda i: (0, i)),
            pl.BlockSpec((MOD_ROWS, tm), lambda i: (0, i)),
        ],
        out_shape=[
            jax.ShapeDtypeStruct((n_rows, d), BF16),
            jax.ShapeDtypeStruct((MOD_ROWS, n_rows), jnp.int32),
            jax.ShapeDtypeStruct((MOD_ROWS, n_rows), F32),
        ],
        compiler_params=_params("arbitrary"),
        name="router",
    )(h, scale, shift, w_cat, bias)


def _gather_rows_kernel(idx_ref, src_ref, dst_ref, sem):
    n = dst_ref.shape[0]

    def copy(r, src_row):
        return pltpu.make_async_copy(src_ref.at[src_row], dst_ref.at[r], sem)

    def start(r):
        copy(r, idx_ref[r]).start()

    def wait(r):
        copy(r, 0).wait()

    for r in range(GATHER_WINDOW):
        start(r)

    def body(r, carry):
        wait(r - GATHER_WINDOW)
        start(r)
        return carry

    lax.fori_loop(GATHER_WINDOW, n, body, 0)
    for r in range(n - GATHER_WINDOW, n):
        wait(r)


def _gather_rows(src, idx):
    n_src, d = src.shape
    n = idx.shape[0]
    sub = d // LANES
    out = pl.pallas_call(
        _gather_rows_kernel,
        grid_spec=pltpu.PrefetchScalarGridSpec(
            num_scalar_prefetch=1,
            grid=(1,),
            in_specs=[pl.BlockSpec(memory_space=pl.ANY)],
            out_specs=pl.BlockSpec(memory_space=pl.ANY),
            scratch_shapes=[pltpu.SemaphoreType.DMA(())],
        ),
        out_shape=jax.ShapeDtypeStruct((n, sub, LANES), src.dtype),
        compiler_params=_params("arbitrary"),
        name="gather_rows",
    )(idx, src.reshape(n_src, sub, LANES))
    return out.reshape(n, d)


def _expert_pair_kernel(elo_ref, ehi_ref, nact_ref, x_ref, wl_ref, wh_ref,
                        g_lo, u_lo, d_lo, g_hi, u_hi, d_hi, o_ref):
    del elo_ref, ehi_ref
    active = pl.program_id(0) < nact_ref[0]

    @pl.when(jnp.logical_not(active))
    def _():
        o_ref[...] = jnp.zeros_like(o_ref)

    @pl.when(active)
    def _():
        x = x_ref[...]

        def ffn(g_ref, u_ref, d_ref):
            a = jnp.dot(x, g_ref[0], preferred_element_type=F32)
            b = jnp.dot(x, u_ref[0], preferred_element_type=F32)
            act = (a * jax.nn.sigmoid(a) * b).astype(BF16)
            return jnp.dot(act, d_ref[0], preferred_element_type=F32)

        y = ffn(g_lo, u_lo, d_lo) * wl_ref[...] + ffn(g_hi, u_hi, d_hi) * wh_ref[...]
        o_ref[...] = y.astype(o_ref.dtype)


def _expert_pairs(xs, row_wlo, row_whi, blk_lo, blk_hi, n_active, w_gate, w_up, w_down, bm):
    n_rows, d = xs.shape
    ff = w_gate.shape[2]
    up_spec = lambda which: pl.BlockSpec((1, d, ff), lambda i, lo, hi, na: ((lo, hi)[which][i], 0, 0))
    dn_spec = lambda which: pl.BlockSpec((1, ff, d), lambda i, lo, hi, na: ((lo, hi)[which][i], 0, 0))
    col = pl.BlockSpec((bm, 1), lambda i, lo, hi, na: (i, 0))
    return pl.pallas_call(
        _expert_pair_kernel,
        grid_spec=pltpu.PrefetchScalarGridSpec(
            num_scalar_prefetch=3,
            grid=(n_rows // bm,),
            in_specs=[pl.BlockSpec((bm, d), lambda i, lo, hi, na: (i, 0)), col, col,
                      up_spec(0), up_spec(0), dn_spec(0), up_spec(1), up_spec(1), dn_spec(1)],
            out_specs=pl.BlockSpec((bm, d), lambda i, lo, hi, na: (i, 0)),
        ),
        out_shape=jax.ShapeDtypeStruct((n_rows, d), BF16),
        compiler_params=_params("arbitrary"),
        name="expert_pairs",
    )(blk_lo, blk_hi, n_active, xs, row_wlo, row_whi, w_gate, w_up, w_down, w_gate, w_up, w_down)


def _gated_add_kernel(h_ref, y_ref, g_ref, o_ref):
    o_ref[...] = h_ref[...] + g_ref[0] * y_ref[...].astype(F32)


def _gated_add(h, y, gate, n_rows, tm, n_lat, n_batch):
    d = h.shape[1]
    mod_map = lambda i: _mod_row_map(tm, n_lat, n_batch)(i, 0)
    return pl.pallas_call(
        _gated_add_kernel,
        grid=(n_rows // tm,),
        in_specs=[pl.BlockSpec((tm, d), lambda i: (i, 0)),
                  pl.BlockSpec((tm, d), lambda i: (i, 0)),
                  pl.BlockSpec((1, 1, d), mod_map)],
        out_specs=pl.BlockSpec((tm, d), lambda i: (i, 0)),
        out_shape=jax.ShapeDtypeStruct((n_rows, d), F32),
        compiler_params=_params("arbitrary"),
        name="gated_add",
    )(h, y, gate)


def _dispatch_plan(cls, bm):
    n = cls.shape[0]
    n_blocks = -(-n // bm) + N_CLASSES
    n_rows = n_blocks * bm
    onehot = (cls[:, None] == jnp.arange(N_CLASSES, dtype=jnp.int32)[None, :]).astype(jnp.int32)
    counts = jnp.sum(onehot, axis=0)
    rank = jnp.sum(jnp.cumsum(onehot, axis=0) * onehot, axis=1) - 1
    blocks_per = (counts + bm - 1) // bm
    block_end = jnp.cumsum(blocks_per)
    row_start = (block_end - blocks_per) * bm
    pos = row_start[cls] + rank
    row_token = jnp.zeros((n_rows,), jnp.int32).at[pos].set(jnp.arange(n, dtype=jnp.int32))
    blk_cls = jnp.minimum(jnp.searchsorted(block_end, jnp.arange(n_blocks, dtype=jnp.int32), side="right"),
                          N_CLASSES - 1)
    rows = jnp.arange(n_rows, dtype=jnp.int32)
    row_cls = jnp.repeat(blk_cls, bm)
    row_valid = (rows - row_start[row_cls]) < counts[row_cls]
    return pos, row_token, row_valid, blk_cls, block_end[-1:].astype(jnp.int32)


def _moe_layer(h, scale, shift, gate, w_group, b_group, w_expert, b_expert, w_gate, w_up, w_down,
               n_rows, tm, n_lat, n_batch):
    v, cls8, wt8 = _router(h, scale, shift, w_group, b_group, w_expert, b_expert, n_rows, tm, n_lat, n_batch)
    cls, w_lo, w_hi = cls8[0], wt8[0], wt8[1]
    pos, row_token, row_valid, blk_cls, n_active = _dispatch_plan(cls, MOE_BLOCK)
    xs = _gather_rows(v, row_token)
    row_wlo = jnp.where(row_valid, w_lo[row_token], 0.0)[:, None]
    row_whi = jnp.where(row_valid, w_hi[row_token], 0.0)[:, None]
    blk_lo = jnp.asarray(_CLASS_LO)[blk_cls]
    blk_hi = jnp.asarray(_CLASS_HI)[blk_cls]
    ys = _expert_pairs(xs, row_wlo, row_whi, blk_lo, blk_hi, n_active, w_gate, w_up, w_down, MOE_BLOCK)
    y = _gather_rows(ys, pos)
    return _gated_add(h, y, gate, n_rows, tm, n_lat, n_batch)


def kernel(x, c, ctx, c_ctx, norm1_w, norm2_w, w_mod, b_mod, even_w_in, even_conv_w, even_w_out,
           odd_w_qkv, odd_q_norm, odd_k_norm, odd_lambda_q1, odd_lambda_k1, odd_lambda_q2, odd_lambda_k2,
           odd_subln_w, odd_w_out, moe_w_group, moe_b_group, moe_w_expert, moe_b_expert,
           moe_w_gate, moe_w_up, moe_w_down):
    n_batch, n_lat, d = x.shape
    n_ctx = ctx.shape[1]
    depth = w_mod.shape[0]
    assert depth == 2 and n_batch < MOD_ROWS and n_lat % n_ctx == 0 and n_lat % GRID_W == 0
    n_lat_rows = n_batch * n_lat
    n_all = n_lat_rows + n_batch * n_ctx
    tm = math.gcd(math.gcd(n_lat, n_batch * n_ctx), 1024)
    tm_seq = math.gcd(n_ctx, 256)

    cond = jnp.zeros((MOD_ROWS, d), F32).at[:n_batch].set(c).at[n_batch].set(c_ctx)
    mod = _modulation(cond, w_mod, b_mod)
    sh1, sc1, g1, sh2, sc2, g2 = [m.reshape(depth, MOD_ROWS, 1, d) for m in jnp.split(mod, 6, axis=-1)]
    scale1 = norm1_w[:, None, None, :] * (1.0 + sc1)
    scale2 = norm2_w[:, None, None, :] * (1.0 + sc2)

    h = jnp.concatenate([x.reshape(n_lat_rows, d), ctx.reshape(n_batch * n_ctx, d)], axis=0)

    p = _ln_matmul(h, scale1[0], sh1[0], even_w_in[0].astype(BF16), n_all, tm, 512, n_lat, n_batch)
    y_conv = _gated_conv(p, even_conv_w[0], n_all, tm_seq, n_lat, n_batch, n_ctx)
    fa, fb = _chan_dft(p, n_all, tm)
    y_four = jnp.concatenate([_seq_dft_lat(fa, fb, n_lat, n_batch, min(512, n_lat), 512),
                              _seq_dft_ctx(fa, fb, n_lat, n_batch, n_ctx)], axis=0)
    h = _proj_residual(y_conv, y_four, 0, 0, even_w_out[0].astype(BF16), h, g1[0], n_all, tm, 1024,
                       n_lat, n_batch)
    h = _moe_layer(h, scale2[0], sh2[0], g2[0], moe_w_group[0], moe_b_group[0], moe_w_expert[0],
                   moe_b_expert[0], moe_w_gate[0].astype(BF16), moe_w_up[0].astype(BF16),
                   moe_w_down[0].astype(BF16), n_all, tm, n_lat, n_batch)

    lam_init = 0.8 - 0.6 * math.exp(-0.3 * 1)
    lam = (jnp.exp(jnp.sum(odd_lambda_q1[0].astype(F32) * odd_lambda_k1[0].astype(F32)))
           - jnp.exp(jnp.sum(odd_lambda_q2[0].astype(F32) * odd_lambda_k2[0].astype(F32))) + lam_init)
    tab = _rope_tables(n_lat, tm, odd_q_norm[0] * (DIFF_HEAD_DIM ** -0.5), odd_k_norm[0])
    qkv = _qkv_proj(h, scale1[1], sh1[1], odd_w_qkv[0].astype(BF16), tab, n_all, tm, 512, n_lat, n_batch)
    subln_gain = (odd_subln_w[0].astype(F32) * (1.0 - lam_init)).reshape(1, DIFF_V_DIM)
    o = _diff_attention(qkv, lam.reshape(1).astype(F32), subln_gain, n_lat, n_batch, n_ctx,
                        min(256, n_lat), min(512, n_lat))
    h = _proj_residual(o, o, 0, 1, odd_w_out[0].astype(BF16), h, g1[1], n_lat_rows, tm, 1024, n_lat, n_batch)
    h = _moe_layer(h, scale2[1], sh2[1], g2[1], moe_w_group[1], moe_b_group[1], moe_w_expert[1],
                   moe_b_expert[1], moe_w_gate[1].astype(BF16), moe_w_up[1].astype(BF16),
                   moe_w_down[1].astype(BF16), n_lat_rows, tm, n_lat, n_batch)
    return h.reshape(n_batch, n_lat, d)
```

```python
import functools
import math

import jax
import jax.numpy as jnp
import numpy as np
from jax import lax
from jax.experimental import pallas as pl
from jax.experimental.pallas import tpu as pltpu

F32 = jnp.float32
BF16 = jnp.bfloat16

GRID_W = 64
EPS = 1e-6
CONV_DIM = 1024
FOURIER_DIM = 1024
FOURIER_GROUPS = 4
FOURIER_GROUP_DIM = FOURIER_DIM // FOURIER_GROUPS
DIFF_HEADS = 8
DIFF_HEAD_DIM = 128
DIFF_V_DIM = 2 * DIFF_HEAD_DIM
QK_DIM = DIFF_HEADS * 2 * DIFF_HEAD_DIM
V_DIM = DIFF_HEADS * DIFF_V_DIM
ROPE_THETA = 10000.0
ROPE_FREQS = DIFF_HEAD_DIM // 4
N_GROUPS = 4
EXPERTS_PER_GROUP = 4
N_EXPERTS = N_GROUPS * EXPERTS_PER_GROUP
EXPERT_FF = 512

LANES = 128
BF16_SUBLANES = 16
MOD_ROWS = 8
VMEM_LIMIT_BYTES = 56 * 1024 * 1024

_PAIRS = ((0, 1), (0, 2), (0, 3), (1, 2), (1, 3), (2, 3))
N_CLASSES = N_GROUPS * len(_PAIRS)
_CLASS_LO = np.array([EXPERTS_PER_GROUP * g + a for g in range(N_GROUPS) for a, _ in _PAIRS], np.int32)
_CLASS_HI = np.array([EXPERTS_PER_GROUP * g + b for g in range(N_GROUPS) for _, b in _PAIRS], np.int32)
ROUTER_COLS = LANES
ROUTER_HALF = 32
MOE_BLOCK = 256


def _params(*semantics):
    return pltpu.CompilerParams(dimension_semantics=semantics, vmem_limit_bytes=VMEM_LIMIT_BYTES)


def _mod_kernel(c_ref, w_ref, b_ref, o_ref):
    c = c_ref[...]
    s = c * jax.nn.sigmoid(c)
    hi = s.astype(BF16)
    lo = (s - hi.astype(F32)).astype(BF16)
    lhs = jnp.concatenate([hi, lo], axis=0)
    acc = jnp.dot(lhs, w_ref[0].astype(BF16), preferred_element_type=F32)
    o_ref[0] = acc[:MOD_ROWS] + acc[MOD_ROWS:] + b_ref[0]


def _modulation(cond, w_mod, b_mod, tn=1024):
    depth, d, n = w_mod.shape
    return pl.pallas_call(
        _mod_kernel,
        grid=(depth, n // tn),
        in_specs=[
            pl.BlockSpec((MOD_ROWS, d), lambda l, j: (0, 0)),
            pl.BlockSpec((1, d, tn), lambda l, j: (l, 0, j)),
            pl.BlockSpec((1, 1, tn), lambda l, j: (l, 0, j)),
        ],
        out_specs=pl.BlockSpec((1, MOD_ROWS, tn), lambda l, j: (l, 0, j)),
        out_shape=jax.ShapeDtypeStruct((depth, MOD_ROWS, n), F32),
        compiler_params=_params("arbitrary", "arbitrary"),
        name="modulation",
    )(cond, w_mod, b_mod.reshape(depth, 1, n))


def _ln_mod(x, scale, shift):
    r = lax.rsqrt(jnp.mean(x * x, axis=-1, keepdims=True) + EPS)
    return x * r * scale + shift


def _mod_row_map(tm, n_lat, n_batch):
    return lambda i, j: (jnp.minimum((i * tm) // n_lat, n_batch), 0, 0)


def _ln_matmul_kernel(h_ref, s_ref, sh_ref, w_ref, o_ref, u_ref):
    @pl.when(pl.program_id(1) == 0)
    def _():
        u_ref[...] = _ln_mod(h_ref[...], s_ref[0], sh_ref[0]).astype(BF16)

    o_ref[...] = jnp.dot(u_ref[...], w_ref[...], preferred_element_type=F32).astype(o_ref.dtype)


def _ln_matmul(h, scale, shift, w, n_rows, tm, tn, n_lat, n_batch):
    d = h.shape[1]
    n = w.shape[1]
    mod_map = _mod_row_map(tm, n_lat, n_batch)
    return pl.pallas_call(
        _ln_matmul_kernel,
        grid=(n_rows // tm, n // tn),
        in_specs=[
            pl.BlockSpec((tm, d), lambda i, j: (i, 0)),
            pl.BlockSpec((1, 1, d), mod_map),
            pl.BlockSpec((1, 1, d), mod_map),
            pl.BlockSpec((d, tn), lambda i, j: (0, j)),
        ],
        out_specs=pl.BlockSpec((tm, tn), lambda i, j: (i, j)),
        out_shape=jax.ShapeDtypeStruct((n_rows, n), BF16),
        scratch_shapes=[pltpu.VMEM((tm, d), BF16)],
        compiler_params=_params("arbitrary", "arbitrary"),
        name="ln_matmul",
    )(h, scale, shift, w)


def _qkv_kernel(h_ref, s_ref, sh_ref, w_ref, tab_ref, o_ref, u_ref, *, n_qk_tiles):
    j = pl.program_id(1)

    @pl.when(j == 0)
    def _():
        u_ref[...] = _ln_mod(h_ref[...], s_ref[0], sh_ref[0]).astype(BF16)

    acc = jnp.dot(u_ref[...], w_ref[...], preferred_element_type=F32)

    @pl.when(j < n_qk_tiles)
    def _():
        cos = tab_ref[0, :, 0:LANES]
        sin_a = tab_ref[0, :, LANES:2 * LANES]
        sin_b = tab_ref[0, :, 2 * LANES:3 * LANES]
        for c in range(acc.shape[1] // DIFF_HEAD_DIM):
            x = acc[:, c * DIFF_HEAD_DIM:(c + 1) * DIFF_HEAD_DIM]
            r = lax.rsqrt(jnp.mean(x * x, axis=-1, keepdims=True) + EPS)
            fwd = pltpu.roll(x, DIFF_HEAD_DIM - ROPE_FREQS, 1)
            bwd = pltpu.roll(x, ROPE_FREQS, 1)
            y = (x * cos + fwd * sin_a + bwd * sin_b) * r
            o_ref[:, c * DIFF_HEAD_DIM:(c + 1) * DIFF_HEAD_DIM] = y.astype(o_ref.dtype)

    @pl.when(j >= n_qk_tiles)
    def _():
        o_ref[...] = acc.astype(o_ref.dtype)


def _rope_tables(n_lat, tm, q_gain, k_gain):
    rows = n_lat // GRID_W
    row = jnp.repeat(jnp.arange(rows, dtype=F32), GRID_W)
    col = jnp.tile(jnp.arange(GRID_W, dtype=F32), rows)
    inv_freq = ROPE_THETA ** (-jnp.arange(ROPE_FREQS, dtype=F32) / ROPE_FREQS)
    ang = jnp.stack([row, col], axis=-1)[:, :, None] * inv_freq
    cos, sin = jnp.cos(ang), jnp.sin(ang)
    zero = jnp.zeros_like(sin)
    cos_l = jnp.stack([cos, cos], axis=2).reshape(n_lat, DIFF_HEAD_DIM)
    sin_a = jnp.stack([-sin, zero], axis=2).reshape(n_lat, DIFF_HEAD_DIM)
    sin_b = jnp.stack([zero, sin], axis=2).reshape(n_lat, DIFF_HEAD_DIM)
    pad = lambda a, v: jnp.concatenate([a, jnp.full((tm, DIFF_HEAD_DIM), v, F32)], axis=0)
    cos_l, sin_a, sin_b = pad(cos_l, 1.0), pad(sin_a, 0.0), pad(sin_b, 0.0)

    def fold(gain):
        g = gain.astype(F32)
        g_fwd = jnp.roll(g, -ROPE_FREQS)
        g_bwd = jnp.roll(g, ROPE_FREQS)
        return jnp.concatenate([cos_l * g, sin_a * g_fwd, sin_b * g_bwd], axis=1)

    return jnp.stack([fold(q_gain), fold(k_gain)], axis=0)


def _qkv_proj(h, scale, shift, w, tab, n_rows, tm, tn, n_lat, n_batch):
    d = h.shape[1]
    n = w.shape[1]
    n_qk_tiles = 2 * QK_DIM // tn
    lat_tiles = n_lat // tm
    n_lat_tiles = n_batch * lat_tiles
    mod_map = _mod_row_map(tm, n_lat, n_batch)

    def tab_map(i, j):
        which = jnp.minimum(j // (QK_DIM // tn), 1)
        return (which, jnp.where(i < n_lat_tiles, i % lat_tiles, lat_tiles), 0)

    return pl.pallas_call(
        functools.partial(_qkv_kernel, n_qk_tiles=n_qk_tiles),
        grid=(n_rows // tm, n // tn),
        in_specs=[
            pl.BlockSpec((tm, d), lambda i, j: (i, 0)),
            pl.BlockSpec((1, 1, d), mod_map),
            pl.BlockSpec((1, 1, d), mod_map),
            pl.BlockSpec((d, tn), lambda i, j: (0, j)),
            pl.BlockSpec((1, tm, 3 * LANES), tab_map),
        ],
        out_specs=pl.BlockSpec((tm, tn), lambda i, j: (i, j)),
        out_shape=jax.ShapeDtypeStruct((n_rows, n), BF16),
        scratch_shapes=[pltpu.VMEM((tm, d), BF16)],
        compiler_params=_params("arbitrary", "arbitrary"),
        name="qkv_proj",
    )(h, scale, shift, w, tab)


def _proj_res_kernel(xa_ref, xb_ref, wa_ref, wb_ref, h_ref, g_ref, o_ref):
    acc = jnp.dot(xa_ref[...], wa_ref[...], preferred_element_type=F32)
    acc += jnp.dot(xb_ref[...], wb_ref[...], preferred_element_type=F32)
    o_ref[...] = h_ref[...] + g_ref[0] * acc


def _proj_residual(xa, xb, col_a, col_b, w, h, gate, n_rows, tm, tn, n_lat, n_batch):
    k, n = w.shape
    kh = k // 2
    mod_map3 = _mod_row_map(tm, n_lat, n_batch)
    return pl.pallas_call(
        _proj_res_kernel,
        grid=(n_rows // tm, n // tn),
        in_specs=[
            pl.BlockSpec((tm, kh), lambda i, j: (i, col_a)),
            pl.BlockSpec((tm, kh), lambda i, j: (i, col_b)),
            pl.BlockSpec((kh, tn), lambda i, j: (0, j)),
            pl.BlockSpec((kh, tn), lambda i, j: (1, j)),
            pl.BlockSpec((tm, tn), lambda i, j: (i, j)),
            pl.BlockSpec((1, 1, tn), lambda i, j: mod_map3(i, j)[:1] + (0, j)),
        ],
        out_specs=pl.BlockSpec((tm, tn), lambda i, j: (i, j)),
        out_shape=jax.ShapeDtypeStruct((n_rows, n), F32),
        compiler_params=_params("arbitrary", "arbitrary"),
        name="proj_residual",
    )(xa, xb, w, w, h, gate)


def _conv_kernel(b_ref, c_ref, v_ref, cp_ref, vp_ref, cn_ref, vn_ref, w_ref, o_ref, *,
                 tm, n_lat, n_lat_rows, n_ctx):
    row0 = pl.program_id(0) * tm
    seq_len = jnp.where(row0 < n_lat_rows, n_lat, n_ctx)
    has_prev = (row0 % seq_len) != 0
    has_next = ((row0 + tm) % seq_len) != 0
    cv = c_ref[...].astype(F32) * v_ref[...].astype(F32)
    last = BF16_SUBLANES - 1
    prev = cp_ref[last:last + 1, :].astype(F32) * vp_ref[last:last + 1, :].astype(F32)
    nxt = cn_ref[0:1, :].astype(F32) * vn_ref[0:1, :].astype(F32)
    prev = jnp.where(has_prev, prev, 0.0)
    nxt = jnp.where(has_next, nxt, 0.0)
    rows = lax.broadcasted_iota(jnp.int32, cv.shape, 0)
    up = jnp.where(rows == 0, prev, pltpu.roll(cv, 1, 0))
    dn = jnp.where(rows == tm - 1, nxt, pltpu.roll(cv, tm - 1, 0))
    w = w_ref[...]
    y = b_ref[...].astype(F32) * (up * w[0:1] + cv * w[1:2] + dn * w[2:3])
    o_ref[...] = y.astype(o_ref.dtype)


def _gated_conv(p, conv_w, n_rows, tm, n_lat, n_batch, n_ctx):
    halo = BF16_SUBLANES
    hb = tm // halo
    n_halo_blocks = n_rows // halo
    w8 = jnp.zeros((MOD_ROWS, CONV_DIM), F32).at[:conv_w.shape[0]].set(conv_w.astype(F32))
    main = lambda col: pl.BlockSpec((tm, CONV_DIM), lambda i: (i, col))
    prev = lambda col: pl.BlockSpec((halo, CONV_DIM), lambda i: (jnp.maximum(i * hb - 1, 0), col))
    nxt = lambda col: pl.BlockSpec((halo, CONV_DIM), lambda i: (jnp.minimum((i + 1) * hb, n_halo_blocks - 1), col))
    return pl.pallas_call(
        functools.partial(_conv_kernel, tm=tm, n_lat=n_lat, n_lat_rows=n_batch * n_lat, n_ctx=n_ctx),
        grid=(n_rows // tm,),
        in_specs=[main(0), main(1), main(2), prev(1), prev(2), nxt(1), nxt(2),
                  pl.BlockSpec((MOD_ROWS, CONV_DIM), lambda i: (0, 0))],
        out_specs=pl.BlockSpec((tm, CONV_DIM), lambda i: (i, 0)),
        out_shape=jax.ShapeDtypeStruct((n_rows, CONV_DIM), BF16),
        compiler_params=_params("arbitrary"),
        name="gated_conv",
    )(p, p, p, p, p, p, p, w8)


def _dft_tables(n):
    k = jnp.arange(n, dtype=jnp.int32)
    ang = ((k[:, None] * k[None, :]) % n).astype(F32) * (2.0 * math.pi / n)
    s = n ** -0.5
    return jnp.cos(ang) * s, jnp.sin(ang) * s


def _seq_dft_tables(n):
    if n % 64 or n <= 64:
        c, s = _dft_tables(n)
        return c.astype(BF16), (-s).astype(BF16)
    m = n // 64
    j1 = jnp.arange(64, dtype=jnp.int32)
    j2 = jnp.arange(m, dtype=jnp.int32)
    k = jnp.arange(n, dtype=jnp.int32)
    a = ((j1[:, None] * k[None, :]) % 64).astype(F32) * (2.0 * math.pi / 64)
    b = ((j2[:, None] * k[None, :]) % n).astype(F32) * (2.0 * math.pi / n)
    ca, sa, cb, sb = jnp.cos(a)[:, None], jnp.sin(a)[:, None], jnp.cos(b)[None], jnp.sin(b)[None]
    scale = n ** -0.5
    c = ((ca * cb - sa * sb) * scale).reshape(n, n)
    s = ((sa * cb + ca * sb) * scale).reshape(n, n)
    return c.astype(BF16), (-s).astype(BF16)


def _chan_dft_kernel(x_ref, cs_ref, a_ref, b_ref):
    gd = FOURIER_GROUP_DIM
    for g in range(FOURIER_GROUPS):
        ab = jnp.dot(x_ref[:, g * gd:(g + 1) * gd], cs_ref[...], preferred_element_type=F32)
        a_ref[:, g * gd:(g + 1) * gd] = ab[:, :gd].astype(a_ref.dtype)
        b_ref[:, g * gd:(g + 1) * gd] = ab[:, gd:].astype(b_ref.dtype)


def _chan_dft(p, n_rows, tm):
    c, s = _dft_tables(FOURIER_GROUP_DIM)
    cs = jnp.concatenate([c, s], axis=1).astype(BF16)
    out = jax.ShapeDtypeStruct((n_rows, FOURIER_DIM), BF16)
    return pl.pallas_call(
        _chan_dft_kernel,
        grid=(n_rows // tm,),
        in_specs=[pl.BlockSpec((tm, FOURIER_DIM), lambda i: (i, 3)),
                  pl.BlockSpec((FOURIER_GROUP_DIM, 2 * FOURIER_GROUP_DIM), lambda i: (0, 0))],
        out_specs=[pl.BlockSpec((tm, FOURIER_DIM), lambda i: (i, 0))] * 2,
        out_shape=[out, out],
        compiler_params=_params("arbitrary"),
        name="chan_dft",
    )(p, cs)


def _seq_dft_kernel(cn_ref, sn_ref, a_ref, b_ref, o_ref):
    acc = jnp.dot(cn_ref[...], a_ref[...], preferred_element_type=F32)
    acc += jnp.dot(sn_ref[...], b_ref[...], preferred_element_type=F32)
    o_ref[...] = acc.astype(o_ref.dtype)


def _seq_dft_lat(a, b, n_lat, n_batch, tm, tn):
    cn, sn = _seq_dft_tables(n_lat)
    n_rows = n_batch * n_lat
    mt = n_lat // tm
    return pl.pallas_call(
        _seq_dft_kernel,
        grid=(n_batch, FOURIER_DIM // tn, mt),
        in_specs=[pl.BlockSpec((tm, n_lat), lambda bi, j, i: (i, 0)),
                  pl.BlockSpec((tm, n_lat), lambda bi, j, i: (i, 0)),
                  pl.BlockSpec((n_lat, tn), lambda bi, j, i: (bi, j)),
                  pl.BlockSpec((n_lat, tn), lambda bi, j, i: (bi, j))],
        out_specs=pl.BlockSpec((tm, tn), lambda bi, j, i: (bi * mt + i, j)),
        out_shape=jax.ShapeDtypeStruct((n_rows, FOURIER_DIM), BF16),
        compiler_params=_params("arbitrary", "arbitrary", "arbitrary"),
        name="seq_dft_lat",
    )(cn, sn, a, b)


def _seq_dft_ctx(a, b, n_lat, n_batch, n_ctx):
    cn, sn = _seq_dft_tables(n_ctx)
    first = n_batch * n_lat // n_ctx
    blk = lambda: pl.BlockSpec((n_ctx, FOURIER_DIM), lambda bi: (first + bi, 0))
    tab = lambda: pl.BlockSpec((n_ctx, n_ctx), lambda bi: (0, 0))
    return pl.pallas_call(
        _seq_dft_kernel,
        grid=(n_batch,),
        in_specs=[tab(), tab(), blk(), blk()],
        out_specs=pl.BlockSpec((n_ctx, FOURIER_DIM), lambda bi: (bi, 0)),
        out_shape=jax.ShapeDtypeStruct((n_batch * n_ctx, FOURIER_DIM), BF16),
        compiler_params=_params("arbitrary"),
        name="seq_dft_ctx",
    )(cn, sn, a, b)


def _attn_kernel(lam_ref, q_ref, kl_ref, kc_ref, vl_ref, vc_ref, g_ref, o_ref, s1_ref, s2_ref, *,
                 n_lat, n_ctx, ck):
    dh = DIFF_HEAD_DIM
    nt = (((1,), (1,)), ((), ()))
    q = q_ref[...]
    q1, q2 = q[:, :dh], q[:, dh:]
    chunks = [(kl_ref, c * ck, ck, c * ck) for c in range(n_lat // ck)] + [(kc_ref, 0, n_ctx, n_lat)]
    for ref, src, size, dst in chunks:
        k = ref[src:src + size, :]
        s1_ref[:, dst:dst + size] = lax.dot_general(q1, k[:, :dh], nt, preferred_element_type=F32)
        s2_ref[:, dst:dst + size] = lax.dot_general(q2, k[:, dh:], nt, preferred_element_type=F32)
    s1 = s1_ref[...]
    s2 = s2_ref[...]
    p1 = jnp.exp(s1 - jnp.max(s1, axis=-1, keepdims=True))
    p2 = jnp.exp(s2 - jnp.max(s2, axis=-1, keepdims=True))
    r1 = 1.0 / jnp.sum(p1, axis=-1, keepdims=True)
    r2 = lam_ref[0] / jnp.sum(p2, axis=-1, keepdims=True)
    a = (p1 * r1 - p2 * r2).astype(BF16)
    o = jnp.dot(a[:, :n_lat], vl_ref[...], preferred_element_type=F32)
    o += jnp.dot(a[:, n_lat:], vc_ref[...], preferred_element_type=F32)
    r = lax.rsqrt(jnp.mean(o * o, axis=-1, keepdims=True) + EPS)
    o_ref[...] = (o * r * g_ref[...]).astype(o_ref.dtype)


def _diff_attention(qkv, lam, subln_gain, n_lat, n_batch, n_ctx, tq, ck):
    hd = DIFF_V_DIM
    qt = n_lat // tq
    k0 = QK_DIM // hd
    v0 = 2 * QK_DIM // hd
    ctx0 = n_batch * n_lat // n_ctx
    return pl.pallas_call(
        functools.partial(_attn_kernel, n_lat=n_lat, n_ctx=n_ctx, ck=ck),
        grid=(n_batch, DIFF_HEADS, qt),
        in_specs=[
            pl.BlockSpec(memory_space=pltpu.SMEM),
            pl.BlockSpec((tq, hd), lambda b, h, i: (b * qt + i, h)),
            pl.BlockSpec((n_lat, hd), lambda b, h, i: (b, k0 + h)),
            pl.BlockSpec((n_ctx, hd), lambda b, h, i: (ctx0 + b, k0 + h)),
            pl.BlockSpec((n_lat, hd), lambda b, h, i: (b, v0 + h)),
            pl.BlockSpec((n_ctx, hd), lambda b, h, i: (ctx0 + b, v0 + h)),
            pl.BlockSpec((1, hd), lambda b, h, i: (0, 0)),
        ],
        out_specs=pl.BlockSpec((tq, hd), lambda b, h, i: (b * qt + i, h)),
        out_shape=jax.ShapeDtypeStruct((n_batch * n_lat, V_DIM), BF16),
        scratch_shapes=[pltpu.VMEM((tq, n_lat + n_ctx), F32), pltpu.VMEM((tq, n_lat + n_ctx), F32)],
        compiler_params=_params("arbitrary", "arbitrary", "arbitrary"),
        name="diff_attention",
    )(lam, qkv, qkv, qkv, qkv, qkv, subln_gain)


def _router_kernel(h_ref, s_ref, sh_ref, w_ref, b_ref, v_ref, cls_ref, wt_ref):
    v = _ln_mod(h_ref[...], s_ref[0], sh_ref[0])
    v_hi = v.astype(BF16)
    v_lo = (v - v_hi.astype(F32)).astype(BF16)
    v_ref[...] = v
    w = w_ref[...]
    lg = jnp.dot(v_hi, w, preferred_element_type=F32) + jnp.dot(v_lo, w, preferred_element_type=F32)
    lgt = lg.T
    lt = lgt[:ROUTER_HALF] + lgt[ROUTER_HALF:2 * ROUTER_HALF] + b_ref[...]

    def row(i):
        return lt[i:i + 1, :]

    best, g_idx = row(0), jnp.zeros_like(row(0), dtype=jnp.int32)
    for g in range(1, N_GROUPS):
        upd = row(g) > best
        g_idx = jnp.where(upd, g, g_idx)
        best = jnp.where(upd, row(g), best)
    denom = sum(jnp.exp(row(g) - best) for g in range(N_GROUPS))
    g_w = 1.0 / denom

    e = []
    for j in range(EXPERTS_PER_GROUP):
        sel = row(N_GROUPS + j)
        for g in range(1, N_GROUPS):
            sel = jnp.where(g_idx == g, row(N_GROUPS + EXPERTS_PER_GROUP * g + j), sel)
        e.append(sel)
    v1, i1 = e[0], jnp.zeros_like(g_idx)
    for j in range(1, EXPERTS_PER_GROUP):
        upd = e[j] > v1
        i1 = jnp.where(upd, j, i1)
        v1 = jnp.where(upd, e[j], v1)
    neg = jnp.float32(-jnp.inf)
    v2, i2 = jnp.full_like(v1, neg), jnp.full_like(i1, -1)
    for j in range(EXPERTS_PER_GROUP):
        upd = (i1 != j) & ((e[j] > v2) | (i2 < 0))
        i2 = jnp.where(upd, j, i2)
        v2 = jnp.where(upd, e[j], v2)
    t = jnp.exp(v2 - v1)
    w1 = g_w / (1.0 + t)
    w2 = g_w * t / (1.0 + t)
    first_lo = i1 < i2
    lo = jnp.where(first_lo, i1, i2)
    hi = jnp.where(first_lo, i2, i1)
    pair = jnp.where(lo == 0, hi - 1, jnp.where(lo == 1, hi + 1, 5))
    cls_ref[...] = jnp.broadcast_to(len(_PAIRS) * g_idx + pair, cls_ref.shape)
    wt_ref[...] = jnp.concatenate(
        [jnp.where(first_lo, w1, w2), jnp.where(first_lo, w2, w1), jnp.zeros((MOD_ROWS - 2, w1.shape[1]), F32)], axis=0)


def _router(h, scale, shift, w_group, b_group, w_expert, b_expert, n_rows, tm, n_lat, n_batch):
    d = h.shape[1]
    n_logits = N_GROUPS + N_EXPERTS
    half = ROUTER_HALF
    w = jnp.concatenate([w_group, w_expert], axis=1).astype(F32)
    w_hi = w.astype(BF16)
    w_lo = (w - w_hi.astype(F32)).astype(BF16)
    pad = lambda a: jnp.pad(a, ((0, 0), (0, half - n_logits)))
    w_cat = jnp.concatenate([pad(w_hi), pad(w_lo), jnp.zeros((d, ROUTER_COLS - 2 * half), BF16)], axis=1)
    bias = jnp.concatenate([b_group, b_expert]).astype(F32)
    bias = jnp.pad(bias, (0, half - n_logits)).reshape(half, 1)
    mod_map = lambda i: _mod_row_map(tm, n_lat, n_batch)(i, 0)
    return pl.pallas_call(
        _router_kernel,
        grid=(n_rows // tm,),
        in_specs=[
            pl.BlockSpec((tm, d), lambda i: (i, 0)),
            pl.BlockSpec((1, 1, d), mod_map),
            pl.BlockSpec((1, 1, d), mod_map),
            pl.BlockSpec((d, ROUTER_COLS), lambda i: (0, 0)),
            pl.BlockSpec((ROUTER_HALF, 1), lambda i: (0, 0)),
        ],
        out_specs=[
            pl.BlockSpec((tm, d), lambda i: (i, 0)),
            pl.BlockSpec((MOD_ROWS, tm), lambda i: (0, i)),
            pl.BlockSpec((MOD_ROWS, tm), lambda i: (0, i)),
        ],
        out_shape=[
            jax.ShapeDtypeStruct((n_rows, d), F32),
            jax.ShapeDtypeStruct((MOD_ROWS, n_rows), jnp.int32),
            jax.ShapeDtypeStruct((MOD_ROWS, n_rows), F32),
        ],
        compiler_params=_params("arbitrary"),
        name="router",
    )(h, scale, shift, w_cat, bias)


def _issue_row_gather(idx_ref, base, src_hbm, buf, sem):
    def body(r, carry):
        pltpu.make_async_copy(src_hbm.at[pl.ds(idx_ref[base + r], 1), :], buf.at[pl.ds(r, 1), :], sem).start()
        return carry

    lax.fori_loop(0, buf.shape[0], body, 0, unroll=8)


def _wait_row_gather(src_hbm, buf, sem):
    pltpu.make_async_copy(src_hbm.at[pl.ds(0, buf.shape[0]), :], buf, sem).wait()


def _expert_pair_kernel(elo_ref, ehi_ref, nact_ref, tok_ref, v_hbm, wl_ref, wh_ref,
                        g_lo, u_lo, d_lo, g_hi, u_hi, d_hi, o_ref, xbuf, sem):
    del elo_ref, ehi_ref
    i = pl.program_id(0)
    bm = o_ref.shape[0]
    slot = i % 2
    n_active = nact_ref[0]

    @pl.when(i == 0)
    def _():
        _issue_row_gather(tok_ref, 0, v_hbm, xbuf.at[0], sem.at[0])

    @pl.when(i + 1 < n_active)
    def _():
        _issue_row_gather(tok_ref, (i + 1) * bm, v_hbm, xbuf.at[1 - slot], sem.at[1 - slot])

    @pl.when(i >= n_active)
    def _():
        o_ref[...] = jnp.zeros_like(o_ref)

    @pl.when(i < n_active)
    def _():
        _wait_row_gather(v_hbm, xbuf.at[slot], sem.at[slot])
        x = xbuf[slot].astype(BF16)

        def ffn(g_ref, u_ref, d_ref):
            a = jnp.dot(x, g_ref[0], preferred_element_type=F32)
            b = jnp.dot(x, u_ref[0], preferred_element_type=F32)
            act = (a * jax.nn.sigmoid(a) * b).astype(BF16)
            return jnp.dot(act, d_ref[0], preferred_element_type=F32)

        o_ref[...] = ffn(g_lo, u_lo, d_lo) * wl_ref[...] + ffn(g_hi, u_hi, d_hi) * wh_ref[...]


def _expert_pairs(v, row_token, row_wlo, row_whi, blk_lo, blk_hi, n_active, w_gate, w_up, w_down, bm):
    d = v.shape[1]
    n_rows = row_token.shape[0]
    ff = w_gate.shape[2]
    up_spec = lambda which: pl.BlockSpec((1, d, ff), lambda i, lo, hi, na, tok: ((lo, hi)[which][i], 0, 0))
    dn_spec = lambda which: pl.BlockSpec((1, ff, d), lambda i, lo, hi, na, tok: ((lo, hi)[which][i], 0, 0))
    col = pl.BlockSpec((bm, 1), lambda i, lo, hi, na, tok: (i, 0))
    return pl.pallas_call(
        _expert_pair_kernel,
        grid_spec=pltpu.PrefetchScalarGridSpec(
            num_scalar_prefetch=4,
            grid=(n_rows // bm,),
            in_specs=[pl.BlockSpec(memory_space=pl.ANY), col, col,
                      up_spec(0), up_spec(0), dn_spec(0), up_spec(1), up_spec(1), dn_spec(1)],
            out_specs=pl.BlockSpec((bm, d), lambda i, lo, hi, na, tok: (i, 0)),
            scratch_shapes=[pltpu.VMEM((2, bm, d), F32), pltpu.SemaphoreType.DMA((2,))],
        ),
        out_shape=jax.ShapeDtypeStruct((n_rows, d), F32),
        compiler_params=_params("arbitrary"),
        name="expert_pairs",
    )(blk_lo, blk_hi, n_active, row_token, v, row_wlo, row_whi, w_gate, w_up, w_down, w_gate, w_up, w_down)


def _gated_gather_add_kernel(pos_ref, h_ref, y_hbm, g_ref, o_ref, ybuf, sem):
    i = pl.program_id(0)
    tm = o_ref.shape[0]
    slot = i % 2

    @pl.when(i == 0)
    def _():
        _issue_row_gather(pos_ref, 0, y_hbm, ybuf.at[0], sem.at[0])

    @pl.when(i + 1 < pl.num_programs(0))
    def _():
        _issue_row_gather(pos_ref, (i + 1) * tm, y_hbm, ybuf.at[1 - slot], sem.at[1 - slot])

    _wait_row_gather(y_hbm, ybuf.at[slot], sem.at[slot])
    o_ref[...] = h_ref[...] + g_ref[0] * ybuf[slot]


def _gated_gather_add(h, ys, pos, gate, n_rows, tm, n_lat, n_batch):
    d = h.shape[1]
    mod_row = _mod_row_map(tm, n_lat, n_batch)
    return pl.pallas_call(
        _gated_gather_add_kernel,
        grid_spec=pltpu.PrefetchScalarGridSpec(
            num_scalar_prefetch=1,
            grid=(n_rows // tm,),
            in_specs=[pl.BlockSpec((tm, d), lambda i, pos: (i, 0)),
                      pl.BlockSpec(memory_space=pl.ANY),
                      pl.BlockSpec((1, 1, d), lambda i, pos: mod_row(i, 0))],
            out_specs=pl.BlockSpec((tm, d), lambda i, pos: (i, 0)),
            scratch_shapes=[pltpu.VMEM((2, tm, d), F32), pltpu.SemaphoreType.DMA((2,))],
        ),
        out_shape=jax.ShapeDtypeStruct((n_rows, d), F32),
        compiler_params=_params("arbitrary"),
        name="gated_gather_add",
    )(pos, h, ys, gate)


def _dispatch_plan(cls, bm):
    n = cls.shape[0]
    n_blocks = -(-n // bm) + N_CLASSES
    n_rows = n_blocks * bm
    onehot = (cls[:, None] == jnp.arange(N_CLASSES, dtype=jnp.int32)[None, :]).astype(jnp.int32)
    counts = jnp.sum(onehot, axis=0)
    rank = jnp.sum(jnp.cumsum(onehot, axis=0) * onehot, axis=1) - 1
    blocks_per = (counts + bm - 1) // bm
    block_end = jnp.cumsum(blocks_per)
    row_start = (block_end - blocks_per) * bm
    pos = row_start[cls] + rank
    row_token = jnp.zeros((n_rows,), jnp.int32).at[pos].set(jnp.arange(n, dtype=jnp.int32))
    blk_cls = jnp.minimum(jnp.searchsorted(block_end, jnp.arange(n_blocks, dtype=jnp.int32), side="right"),
                          N_CLASSES - 1)
    rows = jnp.arange(n_rows, dtype=jnp.int32)
    row_cls = jnp.repeat(blk_cls, bm)
    row_valid = (rows - row_start[row_cls]) < counts[row_cls]
    return pos, row_token, row_valid, blk_cls, block_end[-1:].astype(jnp.int32)


def _moe_layer(h, scale, shift, gate, w_group, b_group, w_expert, b_expert, w_gate, w_up, w_down,
               n_rows, tm, n_lat, n_batch):
    v, cls8, wt8 = _router(h, scale, shift, w_group, b_group, w_expert, b_expert, n_rows, tm, n_lat, n_batch)
    cls, w_lo, w_hi = cls8[0], wt8[0], wt8[1]
    pos, row_token, row_valid, blk_cls, n_active = _dispatch_plan(cls, MOE_BLOCK)
    row_wlo = jnp.where(row_valid, w_lo[row_token], 0.0)[:, None]
    row_whi = jnp.where(row_valid, w_hi[row_token], 0.0)[:, None]
    blk_lo = jnp.asarray(_CLASS_LO)[blk_cls]
    blk_hi = jnp.asarray(_CLASS_HI)[blk_cls]
    ys = _expert_pairs(v, row_token, row_wlo, row_whi, blk_lo, blk_hi, n_active, w_gate, w_up, w_down, MOE_BLOCK)
    return _gated_gather_add(h, ys, pos, gate, n_rows, min(tm, 512), n_lat, n_batch)


def kernel(x, c, ctx, c_ctx, norm1_w, norm2_w, w_mod, b_mod, even_w_in, even_conv_w, even_w_out,
           odd_w_qkv, odd_q_norm, odd_k_norm, odd_lambda_q1, odd_lambda_k1, odd_lambda_q2, odd_lambda_k2,
           odd_subln_w, odd_w_out, moe_w_group, moe_b_group, moe_w_expert, moe_b_expert,
           moe_w_gate, moe_w_up, moe_w_down):
    n_batch, n_lat, d = x.shape
    n_ctx = ctx.shape[1]
    depth = w_mod.shape[0]
    assert depth == 2 and n_batch < MOD_ROWS and n_lat % n_ctx == 0 and n_lat % GRID_W == 0
    n_lat_rows = n_batch * n_lat
    n_all = n_lat_rows + n_batch * n_ctx
    tm = math.gcd(math.gcd(n_lat, n_batch * n_ctx), 1024)
    tm_seq = math.gcd(n_ctx, 256)

    cond = jnp.zeros((MOD_ROWS, d), F32).at[:n_batch].set(c).at[n_batch].set(c_ctx)
    mod = _modulation(cond, w_mod, b_mod)
    sh1, sc1, g1, sh2, sc2, g2 = [m.reshape(depth, MOD_ROWS, 1, d) for m in jnp.split(mod, 6, axis=-1)]
    scale1 = norm1_w[:, None, None, :] * (1.0 + sc1)
    scale2 = norm2_w[:, None, None, :] * (1.0 + sc2)

    h = jnp.concatenate([x.reshape(n_lat_rows, d), ctx.reshape(n_batch * n_ctx, d)], axis=0)

    p = _ln_matmul(h, scale1[0], sh1[0], even_w_in[0].astype(BF16), n_all, tm, 512, n_lat, n_batch)
    y_conv = _gated_conv(p, even_conv_w[0], n_all, tm_seq, n_lat, n_batch, n_ctx)
    fa, fb = _chan_dft(p, n_all, tm)
    y_four = jnp.concatenate([_seq_dft_lat(fa, fb, n_lat, n_batch, min(512, n_lat), 512),
                              _seq_dft_ctx(fa, fb, n_lat, n_batch, n_ctx)], axis=0)
    h = _proj_residual(y_conv, y_four, 0, 0, even_w_out[0].astype(BF16), h, g1[0], n_all, tm, 1024,
                       n_lat, n_batch)
    h = _moe_layer(h, scale2[0], sh2[0], g2[0], moe_w_group[0], moe_b_group[0], moe_w_expert[0],
                   moe_b_expert[0], moe_w_gate[0].astype(BF16), moe_w_up[0].astype(BF16),
                   moe_w_down[0].astype(BF16), n_all, tm, n_lat, n_batch)

    lam_init = 0.8 - 0.6 * math.exp(-0.3 * 1)
    lam = (jnp.exp(jnp.sum(odd_lambda_q1[0].astype(F32) * odd_lambda_k1[0].astype(F32)))
           - jnp.exp(jnp.sum(odd_lambda_q2[0].astype(F32) * odd_lambda_k2[0].astype(F32))) + lam_init)
    tab = _rope_tables(n_lat, tm, odd_q_norm[0] * (DIFF_HEAD_DIM ** -0.5), odd_k_norm[0])
    qkv = _qkv_proj(h, scale1[1], sh1[1], odd_w_qkv[0].astype(BF16), tab, n_all, tm, 512, n_lat, n_batch)
    subln_gain = (odd_subln_w[0].astype(F32) * (1.0 - lam_init)).reshape(1, DIFF_V_DIM)
    o = _diff_attention(qkv, lam.reshape(1).astype(F32), subln_gain, n_lat, n_batch, n_ctx,
                        min(256, n_lat), min(512, n_lat))
    h = _proj_residual(o, o, 0, 1, odd_w_out[0].astype(BF16), h, g1[1], n_lat_rows, tm, 1024, n_lat, n_batch)
    h = _moe_layer(h, scale2[1], sh2[1], g2[1], moe_w_group[1], moe_b_group[1], moe_w_expert[1],
                   moe_b_expert[1], moe_w_gate[1].astype(BF16), moe_w_up[1].astype(BF16),
                   moe_w_down[1].astype(BF16), n_lat_rows, tm, n_lat, n_batch)
    return h.reshape(n_batch, n_lat, d)
```

```python
import functools
import math

import jax
import jax.numpy as jnp
import numpy as np
from jax import lax
from jax.experimental import pallas as pl
from jax.experimental.pallas import tpu as pltpu

F32 = jnp.float32
BF16 = jnp.bfloat16

GRID_W = 64
EPS = 1e-6
CONV_DIM = 1024
FOURIER_DIM = 1024
FOURIER_GROUPS = 4
FOURIER_GROUP_DIM = FOURIER_DIM // FOURIER_GROUPS
DIFF_HEADS = 8
DIFF_HEAD_DIM = 128
DIFF_V_DIM = 2 * DIFF_HEAD_DIM
QK_DIM = DIFF_HEADS * 2 * DIFF_HEAD_DIM
V_DIM = DIFF_HEADS * DIFF_V_DIM
ROPE_THETA = 10000.0
ROPE_FREQS = DIFF_HEAD_DIM // 4
N_GROUPS = 4
EXPERTS_PER_GROUP = 4
N_EXPERTS = N_GROUPS * EXPERTS_PER_GROUP
EXPERT_FF = 512

LANES = 128
BF16_SUBLANES = 16
MOD_ROWS = 8
VMEM_LIMIT_BYTES = 56 * 1024 * 1024

_PAIRS = ((0, 1), (0, 2), (0, 3), (1, 2), (1, 3), (2, 3))
N_CLASSES = N_GROUPS * len(_PAIRS)
_CLASS_LO = np.array([EXPERTS_PER_GROUP * g + a for g in range(N_GROUPS) for a, _ in _PAIRS], np.int32)
_CLASS_HI = np.array([EXPERTS_PER_GROUP * g + b for g in range(N_GROUPS) for _, b in _PAIRS], np.int32)
ROUTER_COLS = LANES
ROUTER_HALF = 32
MOE_BLOCK = 256


def _params(*semantics):
    return pltpu.CompilerParams(dimension_semantics=semantics, vmem_limit_bytes=VMEM_LIMIT_BYTES)


def _mod_kernel(c_ref, w_ref, b_ref, o_ref):
    c = c_ref[...]
    s = c * jax.nn.sigmoid(c)
    hi = s.astype(BF16)
    lo = (s - hi.astype(F32)).astype(BF16)
    lhs = jnp.concatenate([hi, lo], axis=0)
    acc = jnp.dot(lhs, w_ref[0].astype(BF16), preferred_element_type=F32)
    o_ref[0] = acc[:MOD_ROWS] + acc[MOD_ROWS:] + b_ref[0]


def _modulation(cond, w_mod, b_mod, tn=1024):
    depth, d, n = w_mod.shape
    return pl.pallas_call(
        _mod_kernel,
        grid=(depth, n // tn),
        in_specs=[
            pl.BlockSpec((MOD_ROWS, d), lambda l, j: (0, 0)),
            pl.BlockSpec((1, d, tn), lambda l, j: (l, 0, j)),
            pl.BlockSpec((1, 1, tn), lambda l, j: (l, 0, j)),
        ],
        out_specs=pl.BlockSpec((1, MOD_ROWS, tn), lambda l, j: (l, 0, j)),
        out_shape=jax.ShapeDtypeStruct((depth, MOD_ROWS, n), F32),
        compiler_params=_params("arbitrary", "arbitrary"),
        name="modulation",
    )(cond, w_mod, b_mod.reshape(depth, 1, n))


def _ln_mod(x, scale, shift):
    r = lax.rsqrt(jnp.mean(x * x, axis=-1, keepdims=True) + EPS)
    return x * r * scale + shift


def _mod_row_map(tm, n_lat, n_batch):
    return lambda i, j: (jnp.minimum((i * tm) // n_lat, n_batch), 0, 0)


def _ln_matmul_kernel(h_ref, s_ref, sh_ref, w_ref, o_ref, u_ref):
    @pl.when(pl.program_id(1) == 0)
    def _():
        u_ref[...] = _ln_mod(h_ref[...], s_ref[0], sh_ref[0]).astype(BF16)

    o_ref[...] = jnp.dot(u_ref[...], w_ref[...], preferred_element_type=F32).astype(o_ref.dtype)


def _ln_matmul(h, scale, shift, w, n_rows, tm, tn, n_lat, n_batch):
    d = h.shape[1]
    n = w.shape[1]
    mod_map = _mod_row_map(tm, n_lat, n_batch)
    return pl.pallas_call(
        _ln_matmul_kernel,
        grid=(n_rows // tm, n // tn),
        in_specs=[
            pl.BlockSpec((tm, d), lambda i, j: (i, 0)),
            pl.BlockSpec((1, 1, d), mod_map),
            pl.BlockSpec((1, 1, d), mod_map),
            pl.BlockSpec((d, tn), lambda i, j: (0, j)),
        ],
        out_specs=pl.BlockSpec((tm, tn), lambda i, j: (i, j)),
        out_shape=jax.ShapeDtypeStruct((n_rows, n), BF16),
        scratch_shapes=[pltpu.VMEM((tm, d), BF16)],
        compiler_params=_params("arbitrary", "arbitrary"),
        name="ln_matmul",
    )(h, scale, shift, w)


def _qkv_kernel(h_ref, s_ref, sh_ref, w_ref, tab_ref, o_ref, u_ref, *, n_qk_tiles):
    j = pl.program_id(1)

    @pl.when(j == 0)
    def _():
        u_ref[...] = _ln_mod(h_ref[...], s_ref[0], sh_ref[0]).astype(BF16)

    acc = jnp.dot(u_ref[...], w_ref[...], preferred_element_type=F32)

    @pl.when(j < n_qk_tiles)
    def _():
        cos = tab_ref[0, :, 0:LANES]
        sin_a = tab_ref[0, :, LANES:2 * LANES]
        sin_b = tab_ref[0, :, 2 * LANES:3 * LANES]
        for c in range(acc.shape[1] // DIFF_HEAD_DIM):
            x = acc[:, c * DIFF_HEAD_DIM:(c + 1) * DIFF_HEAD_DIM]
            r = lax.rsqrt(jnp.mean(x * x, axis=-1, keepdims=True) + EPS)
            fwd = pltpu.roll(x, DIFF_HEAD_DIM - ROPE_FREQS, 1)
            bwd = pltpu.roll(x, ROPE_FREQS, 1)
            y = (x * cos + fwd * sin_a + bwd * sin_b) * r
            o_ref[:, c * DIFF_HEAD_DIM:(c + 1) * DIFF_HEAD_DIM] = y.astype(o_ref.dtype)

    @pl.when(j >= n_qk_tiles)
    def _():
        o_ref[...] = acc.astype(o_ref.dtype)


def _rope_tables(n_lat, tm, q_gain, k_gain):
    rows = n_lat // GRID_W
    row = jnp.repeat(jnp.arange(rows, dtype=F32), GRID_W)
    col = jnp.tile(jnp.arange(GRID_W, dtype=F32), rows)
    inv_freq = ROPE_THETA ** (-jnp.arange(ROPE_FREQS, dtype=F32) / ROPE_FREQS)
    ang = jnp.stack([row, col], axis=-1)[:, :, None] * inv_freq
    cos, sin = jnp.cos(ang), jnp.sin(ang)
    zero = jnp.zeros_like(sin)
    cos_l = jnp.stack([cos, cos], axis=2).reshape(n_lat, DIFF_HEAD_DIM)
    sin_a = jnp.stack([-sin, zero], axis=2).reshape(n_lat, DIFF_HEAD_DIM)
    sin_b = jnp.stack([zero, sin], axis=2).reshape(n_lat, DIFF_HEAD_DIM)
    pad = lambda a, v: jnp.concatenate([a, jnp.full((tm, DIFF_HEAD_DIM), v, F32)], axis=0)
    cos_l, sin_a, sin_b = pad(cos_l, 1.0), pad(sin_a, 0.0), pad(sin_b, 0.0)

    def fold(gain):
        g = gain.astype(F32)
        g_fwd = jnp.roll(g, -ROPE_FREQS)
        g_bwd = jnp.roll(g, ROPE_FREQS)
        return jnp.concatenate([cos_l * g, sin_a * g_fwd, sin_b * g_bwd], axis=1)

    return jnp.stack([fold(q_gain), fold(k_gain)], axis=0)


def _qkv_proj(h, scale, shift, w, tab, n_rows, tm, tn, n_lat, n_batch):
    d = h.shape[1]
    n = w.shape[1]
    n_qk_tiles = 2 * QK_DIM // tn
    lat_tiles = n_lat // tm
    n_lat_tiles = n_batch * lat_tiles
    mod_map = _mod_row_map(tm, n_lat, n_batch)

    def tab_map(i, j):
        which = jnp.minimum(j // (QK_DIM // tn), 1)
        return (which, jnp.where(i < n_lat_tiles, i % lat_tiles, lat_tiles), 0)

    return pl.pallas_call(
        functools.partial(_qkv_kernel, n_qk_tiles=n_qk_tiles),
        grid=(n_rows // tm, n // tn),
        in_specs=[
            pl.BlockSpec((tm, d), lambda i, j: (i, 0)),
            pl.BlockSpec((1, 1, d), mod_map),
            pl.BlockSpec((1, 1, d), mod_map),
            pl.BlockSpec((d, tn), lambda i, j: (0, j)),
            pl.BlockSpec((1, tm, 3 * LANES), tab_map),
        ],
        out_specs=pl.BlockSpec((tm, tn), lambda i, j: (i, j)),
        out_shape=jax.ShapeDtypeStruct((n_rows, n), BF16),
        scratch_shapes=[pltpu.VMEM((tm, d), BF16)],
        compiler_params=_params("arbitrary", "arbitrary"),
        name="qkv_proj",
    )(h, scale, shift, w, tab)


def _proj_res_kernel(xa_ref, xb_ref, wa_ref, wb_ref, h_ref, g_ref, o_ref):
    acc = jnp.dot(xa_ref[...], wa_ref[...], preferred_element_type=F32)
    acc += jnp.dot(xb_ref[...], wb_ref[...], preferred_element_type=F32)
    o_ref[...] = h_ref[...] + g_ref[0] * acc


def _proj_residual(xa, xb, col_a, col_b, w, h, gate, n_rows, tm, tn, n_lat, n_batch):
    k, n = w.shape
    kh = k // 2
    mod_map3 = _mod_row_map(tm, n_lat, n_batch)
    return pl.pallas_call(
        _proj_res_kernel,
        grid=(n_rows // tm, n // tn),
        in_specs=[
            pl.BlockSpec((tm, kh), lambda i, j: (i, col_a)),
            pl.BlockSpec((tm, kh), lambda i, j: (i, col_b)),
            pl.BlockSpec((kh, tn), lambda i, j: (0, j)),
            pl.BlockSpec((kh, tn), lambda i, j: (1, j)),
            pl.BlockSpec((tm, tn), lambda i, j: (i, j)),
            pl.BlockSpec((1, 1, tn), lambda i, j: mod_map3(i, j)[:1] + (0, j)),
        ],
        out_specs=pl.BlockSpec((tm, tn), lambda i, j: (i, j)),
        out_shape=jax.ShapeDtypeStruct((n_rows, n), F32),
        compiler_params=_params("arbitrary", "arbitrary"),
        name="proj_residual",
    )(xa, xb, w, w, h, gate)


def _conv_kernel(b_ref, c_ref, v_ref, cp_ref, vp_ref, cn_ref, vn_ref, w_ref, o_ref, *,
                 tm, n_lat, n_lat_rows, n_ctx):
    row0 = pl.program_id(0) * tm
    seq_len = jnp.where(row0 < n_lat_rows, n_lat, n_ctx)
    has_prev = (row0 % seq_len) != 0
    has_next = ((row0 + tm) % seq_len) != 0
    cv = c_ref[...].astype(F32) * v_ref[...].astype(F32)
    last = BF16_SUBLANES - 1
    prev = cp_ref[last:last + 1, :].astype(F32) * vp_ref[last:last + 1, :].astype(F32)
    nxt = cn_ref[0:1, :].astype(F32) * vn_ref[0:1, :].astype(F32)
    prev = jnp.where(has_prev, prev, 0.0)
    nxt = jnp.where(has_next, nxt, 0.0)
    rows = lax.broadcasted_iota(jnp.int32, cv.shape, 0)
    up = jnp.where(rows == 0, prev, pltpu.roll(cv, 1, 0))
    dn = jnp.where(rows == tm - 1, nxt, pltpu.roll(cv, tm - 1, 0))
    w = w_ref[...]
    y = b_ref[...].astype(F32) * (up * w[0:1] + cv * w[1:2] + dn * w[2:3])
    o_ref[...] = y.astype(o_ref.dtype)


def _gated_conv(p, conv_w, n_rows, tm, n_lat, n_batch, n_ctx):
    halo = BF16_SUBLANES
    hb = tm // halo
    n_halo_blocks = n_rows // halo
    w8 = jnp.zeros((MOD_ROWS, CONV_DIM), F32).at[:conv_w.shape[0]].set(conv_w.astype(F32))
    main = lambda col: pl.BlockSpec((tm, CONV_DIM), lambda i: (i, col))
    prev = lambda col: pl.BlockSpec((halo, CONV_DIM), lambda i: (jnp.maximum(i * hb - 1, 0), col))
    nxt = lambda col: pl.BlockSpec((halo, CONV_DIM), lambda i: (jnp.minimum((i + 1) * hb, n_halo_blocks - 1), col))
    return pl.pallas_call(
        functools.partial(_conv_kernel, tm=tm, n_lat=n_lat, n_lat_rows=n_batch * n_lat, n_ctx=n_ctx),
        grid=(n_rows // tm,),
        in_specs=[main(0), main(1), main(2), prev(1), prev(2), nxt(1), nxt(2),
                  pl.BlockSpec((MOD_ROWS, CONV_DIM), lambda i: (0, 0))],
        out_specs=pl.BlockSpec((tm, CONV_DIM), lambda i: (i, 0)),
        out_shape=jax.ShapeDtypeStruct((n_rows, CONV_DIM), BF16),
        compiler_params=_params("arbitrary"),
        name="gated_conv",
    )(p, p, p, p, p, p, p, w8)


def _dft_tables(n):
    k = jnp.arange(n, dtype=jnp.int32)
    ang = ((k[:, None] * k[None, :]) % n).astype(F32) * (2.0 * math.pi / n)
    s = n ** -0.5
    return jnp.cos(ang) * s, jnp.sin(ang) * s


def _seq_dft_tables(n):
    if n % 64 or n <= 64:
        c, s = _dft_tables(n)
        return c.astype(BF16), (-s).astype(BF16)
    m = n // 64
    j1 = jnp.arange(64, dtype=jnp.int32)
    j2 = jnp.arange(m, dtype=jnp.int32)
    k = jnp.arange(n, dtype=jnp.int32)
    a = ((j1[:, None] * k[None, :]) % 64).astype(F32) * (2.0 * math.pi / 64)
    b = ((j2[:, None] * k[None, :]) % n).astype(F32) * (2.0 * math.pi / n)
    ca, sa, cb, sb = jnp.cos(a)[:, None], jnp.sin(a)[:, None], jnp.cos(b)[None], jnp.sin(b)[None]
    scale = n ** -0.5
    c = ((ca * cb - sa * sb) * scale).reshape(n, n)
    s = ((sa * cb + ca * sb) * scale).reshape(n, n)
    return c.astype(BF16), (-s).astype(BF16)


def _chan_dft_kernel(x_ref, cs_ref, a_ref, b_ref):
    gd = FOURIER_GROUP_DIM
    for g in range(FOURIER_GROUPS):
        ab = jnp.dot(x_ref[:, g * gd:(g + 1) * gd], cs_ref[...], preferred_element_type=F32)
        a_ref[:, g * gd:(g + 1) * gd] = ab[:, :gd].astype(a_ref.dtype)
        b_ref[:, g * gd:(g + 1) * gd] = ab[:, gd:].astype(b_ref.dtype)


def _chan_dft(p, n_rows, tm):
    c, s = _dft_tables(FOURIER_GROUP_DIM)
    cs = jnp.concatenate([c, s], axis=1).astype(BF16)
    out = jax.ShapeDtypeStruct((n_rows, FOURIER_DIM), BF16)
    return pl.pallas_call(
        _chan_dft_kernel,
        grid=(n_rows // tm,),
        in_specs=[pl.BlockSpec((tm, FOURIER_DIM), lambda i: (i, 3)),
                  pl.BlockSpec((FOURIER_GROUP_DIM, 2 * FOURIER_GROUP_DIM), lambda i: (0, 0))],
        out_specs=[pl.BlockSpec((tm, FOURIER_DIM), lambda i: (i, 0))] * 2,
        out_shape=[out, out],
        compiler_params=_params("arbitrary"),
        name="chan_dft",
    )(p, cs)


def _seq_dft_kernel(cn_ref, sn_ref, a_ref, b_ref, o_ref):
    acc = jnp.dot(cn_ref[...], a_ref[...], preferred_element_type=F32)
    acc += jnp.dot(sn_ref[...], b_ref[...], preferred_element_type=F32)
    o_ref[...] = acc.astype(o_ref.dtype)


def _seq_dft_lat(a, b, n_lat, n_batch, tm, tn):
    cn, sn = _seq_dft_tables(n_lat)
    n_rows = n_batch * n_lat
    mt = n_lat // tm
    return pl.pallas_call(
        _seq_dft_kernel,
        grid=(n_batch, FOURIER_DIM // tn, mt),
        in_specs=[pl.BlockSpec((tm, n_lat), lambda bi, j, i: (i, 0)),
                  pl.BlockSpec((tm, n_lat), lambda bi, j, i: (i, 0)),
                  pl.BlockSpec((n_lat, tn), lambda bi, j, i: (bi, j)),
                  pl.BlockSpec((n_lat, tn), lambda bi, j, i: (bi, j))],
        out_specs=pl.BlockSpec((tm, tn), lambda bi, j, i: (bi * mt + i, j)),
        out_shape=jax.ShapeDtypeStruct((n_rows, FOURIER_DIM), BF16),
        compiler_params=_params("arbitrary", "arbitrary", "arbitrary"),
        name="seq_dft_lat",
    )(cn, sn, a, b)


def _seq_dft_ctx(a, b, n_lat, n_batch, n_ctx):
    cn, sn = _seq_dft_tables(n_ctx)
    first = n_batch * n_lat // n_ctx
    blk = lambda: pl.BlockSpec((n_ctx, FOURIER_DIM), lambda bi: (first + bi, 0))
    tab = lambda: pl.BlockSpec((n_ctx, n_ctx), lambda bi: (0, 0))
    return pl.pallas_call(
        _seq_dft_kernel,
        grid=(n_batch,),
        in_specs=[tab(), tab(), blk(), blk()],
        out_specs=pl.BlockSpec((n_ctx, FOURIER_DIM), lambda bi: (bi, 0)),
        out_shape=jax.ShapeDtypeStruct((n_batch * n_ctx, FOURIER_DIM), BF16),
        compiler_params=_params("arbitrary"),
        name="seq_dft_ctx",
    )(cn, sn, a, b)


def _lane_fold(x, op):
    parts = [x[:, c * LANES:(c + 1) * LANES] for c in range(x.shape[1] // LANES)]
    while len(parts) > 1:
        parts = [op(parts[i], parts[i + 1]) if i + 1 < len(parts) else parts[i] for i in range(0, len(parts), 2)]
    return parts[0]


def _attn_kernel(lam_ref, q_ref, kl_ref, kc_ref, vl_ref, vc_ref, g_ref, o_ref,
                 s1a_ref, s2a_ref, s1b_ref, s2b_ref, ma_ref, mb_ref, *, n_lat, n_ctx, ck, tq):
    dh = DIFF_HEAD_DIM
    nt = (((1,), (1,)), ((), ()))
    chunks = [(kl_ref, vl_ref, c * ck, ck, c * ck) for c in range(n_lat // ck)] + [(kc_ref, vc_ref, 0, n_ctx, n_lat)]
    s_slots = ((s1a_ref, s2a_ref), (s1b_ref, s2b_ref))
    m_slots = (ma_ref, mb_ref)
    n_tiles = n_lat // tq

    def rows(t):
        return pl.ds(pl.multiple_of(t * tq, tq), tq)

    def load_q(t):
        q = q_ref[rows(t), :]
        return q[:, :dh], q[:, dh:]

    def score_chunk(qs, slot, chunk, m):
        k_ref, _, src, size, dst = chunk
        k = k_ref[src:src + size, :]
        for c in range(2):
            s = lax.dot_general(qs[c], k[:, c * dh:(c + 1) * dh], nt, preferred_element_type=F32)
            s_slots[slot][c][:, dst:dst + size] = s
            sm = _lane_fold(s, jnp.maximum)
            m[c] = sm if m[c] is None else jnp.maximum(m[c], sm)

    def store_max(slot, m):
        for c in range(2):
            m_slots[slot][c] = jnp.broadcast_to(jnp.max(m[c], axis=-1, keepdims=True), (tq, LANES))

    def exp_chunk(slot, chunk, mb, l):
        _, _, _, size, dst = chunk
        for c in range(2):
            shift = jnp.concatenate([mb[c]] * (size // LANES), axis=1)
            p = jnp.exp2(s_slots[slot][c][:, dst:dst + size] - shift)
            s_slots[slot][c][:, dst:dst + size] = p
            ps = _lane_fold(p, jnp.add)
            l[c] = ps if l[c] is None else l[c] + ps

    def weighted_values(t, slot, l):
        l = [jnp.sum(lc, axis=-1, keepdims=True) for lc in l]
        coef = lam_ref[0] * l[0] / l[1]
        acc = None
        for _, v_ref, src, size, dst in chunks:
            p1, p2 = (ref[:, dst:dst + size] for ref in s_slots[slot])
            a = (p1 - coef * p2).astype(BF16)
            pv = jnp.dot(a, v_ref[src:src + size, :], preferred_element_type=F32)
            acc = pv if acc is None else acc + pv
        o = acc * (1.0 / l[0])
        r = lax.rsqrt(jnp.mean(o * o, axis=-1, keepdims=True) + EPS)
        o_ref[rows(t), :] = (o * r * g_ref[...]).astype(o_ref.dtype)

    def tile(t, slot, has_next):
        mb = [m_slots[slot][0], m_slots[slot][1]]
        l, m = [None, None], [None, None]
        if has_next:
            qs = load_q(t + 1)
        for chunk in chunks:
            if has_next:
                score_chunk(qs, 1 - slot, chunk, m)
            exp_chunk(slot, chunk, mb, l)
        if has_next:
            store_max(1 - slot, m)
        weighted_values(t, slot, l)

    m = [None, None]
    qs = load_q(0)
    for chunk in chunks:
        score_chunk(qs, 0, chunk, m)
    store_max(0, m)

    def body(u, carry):
        tile(2 * u, 0, True)
        tile(2 * u + 1, 1, True)
        return carry

    lax.fori_loop(0, n_tiles // 2 - 1, body, 0)
    tile(n_tiles - 2, 0, True)
    tile(n_tiles - 1, 1, False)


def _diff_attention(qkv, lam, subln_gain, n_lat, n_batch, n_ctx, tq, ck):
    hd = DIFF_V_DIM
    k0 = QK_DIM // hd
    v0 = 2 * QK_DIM // hd
    ctx0 = n_batch * n_lat // n_ctx
    n_keys = n_lat + n_ctx
    return pl.pallas_call(
        functools.partial(_attn_kernel, n_lat=n_lat, n_ctx=n_ctx, ck=ck, tq=tq),
        grid=(n_batch, DIFF_HEADS),
        in_specs=[
            pl.BlockSpec(memory_space=pltpu.SMEM),
            pl.BlockSpec((n_lat, hd), lambda b, h: (b, h)),
            pl.BlockSpec((n_lat, hd), lambda b, h: (b, k0 + h)),
            pl.BlockSpec((n_ctx, hd), lambda b, h: (ctx0 + b, k0 + h)),
            pl.BlockSpec((n_lat, hd), lambda b, h: (b, v0 + h)),
            pl.BlockSpec((n_ctx, hd), lambda b, h: (ctx0 + b, v0 + h)),
            pl.BlockSpec((1, hd), lambda b, h: (0, 0)),
        ],
        out_specs=pl.BlockSpec((n_lat, hd), lambda b, h: (b, h)),
        out_shape=jax.ShapeDtypeStruct((n_batch * n_lat, V_DIM), BF16),
        scratch_shapes=[pltpu.VMEM((tq, n_keys), F32)] * 4 + [pltpu.VMEM((2, tq, LANES), F32)] * 2,
        compiler_params=_params("arbitrary", "arbitrary"),
        name="diff_attention",
    )(lam, qkv, qkv, qkv, qkv, qkv, subln_gain)


def _router_kernel(h_ref, s_ref, sh_ref, w_ref, b_ref, v_ref, cls_ref, wt_ref):
    v = _ln_mod(h_ref[...], s_ref[0], sh_ref[0])
    v_hi = v.astype(BF16)
    v_lo = (v - v_hi.astype(F32)).astype(BF16)
    v_ref[...] = v
    w = w_ref[...]
    lg = jnp.dot(v_hi, w, preferred_element_type=F32) + jnp.dot(v_lo, w, preferred_element_type=F32)
    lgt = lg.T
    lt = lgt[:ROUTER_HALF] + lgt[ROUTER_HALF:2 * ROUTER_HALF] + b_ref[...]

    def row(i):
        return lt[i:i + 1, :]

    best, g_idx = row(0), jnp.zeros_like(row(0), dtype=jnp.int32)
    for g in range(1, N_GROUPS):
        upd = row(g) > best
        g_idx = jnp.where(upd, g, g_idx)
        best = jnp.where(upd, row(g), best)
    denom = sum(jnp.exp(row(g) - best) for g in range(N_GROUPS))
    g_w = 1.0 / denom

    e = []
    for j in range(EXPERTS_PER_GROUP):
        sel = row(N_GROUPS + j)
        for g in range(1, N_GROUPS):
            sel = jnp.where(g_idx == g, row(N_GROUPS + EXPERTS_PER_GROUP * g + j), sel)
        e.append(sel)
    v1, i1 = e[0], jnp.zeros_like(g_idx)
    for j in range(1, EXPERTS_PER_GROUP):
        upd = e[j] > v1
        i1 = jnp.where(upd, j, i1)
        v1 = jnp.where(upd, e[j], v1)
    neg = jnp.float32(-jnp.inf)
    v2, i2 = jnp.full_like(v1, neg), jnp.full_like(i1, -1)
    for j in range(EXPERTS_PER_GROUP):
        upd = (i1 != j) & ((e[j] > v2) | (i2 < 0))
        i2 = jnp.where(upd, j, i2)
        v2 = jnp.where(upd, e[j], v2)
    t = jnp.exp(v2 - v1)
    w1 = g_w / (1.0 + t)
    w2 = g_w * t / (1.0 + t)
    first_lo = i1 < i2
    lo = jnp.where(first_lo, i1, i2)
    hi = jnp.where(first_lo, i2, i1)
    pair = jnp.where(lo == 0, hi - 1, jnp.where(lo == 1, hi + 1, 5))
    cls_ref[...] = jnp.broadcast_to(len(_PAIRS) * g_idx + pair, cls_ref.shape)
    wt_ref[...] = jnp.concatenate(
        [jnp.where(first_lo, w1, w2), jnp.where(first_lo, w2, w1), jnp.zeros((MOD_ROWS - 2, w1.shape[1]), F32)], axis=0)


def _router(h, scale, shift, w_group, b_group, w_expert, b_expert, n_rows, tm, n_lat, n_batch):
    d = h.shape[1]
    n_logits = N_GROUPS + N_EXPERTS
    half = ROUTER_HALF
    w = jnp.concatenate([w_group, w_expert], axis=1).astype(F32)
    w_hi = w.astype(BF16)
    w_lo = (w - w_hi.astype(F32)).astype(BF16)
    pad = lambda a: jnp.pad(a, ((0, 0), (0, half - n_logits)))
    w_cat = jnp.concatenate([pad(w_hi), pad(w_lo), jnp.zeros((d, ROUTER_COLS - 2 * half), BF16)], axis=1)
    bias = jnp.concatenate([b_group, b_expert]).astype(F32)
    bias = jnp.pad(bias, (0, half - n_logits)).reshape(half, 1)
    mod_map = lambda i: _mod_row_map(tm, n_lat, n_batch)(i, 0)
    return pl.pallas_call(
        _router_kernel,
        grid=(n_rows // tm,),
        in_specs=[
            pl.BlockSpec((tm, d), lambda i: (i, 0)),
            pl.BlockSpec((1, 1, d), mod_map),
            pl.BlockSpec((1, 1, d), mod_map),
            pl.BlockSpec((d, ROUTER_COLS), lambda i: (0, 0)),
            pl.BlockSpec((ROUTER_HALF, 1), lambda i: (0, 0)),
        ],
        out_specs=[
            pl.BlockSpec((tm, d), lambda i: (i, 0)),
            pl.BlockSpec((MOD_ROWS, tm), lambda i: (0, i)),
            pl.BlockSpec((MOD_ROWS, tm), lambda i: (0, i)),
        ],
        out_shape=[
            jax.ShapeDtypeStruct((n_rows, d), F32),
            jax.ShapeDtypeStruct((MOD_ROWS, n_rows), jnp.int32),
            jax.ShapeDtypeStruct((MOD_ROWS, n_rows), F32),
        ],
        compiler_params=_params("arbitrary"),
        name="router",
    )(h, scale, shift, w_cat, bias)


def _issue_row_gather(idx_ref, base, src_hbm, buf, sem):
    def body(r, carry):
        pltpu.make_async_copy(src_hbm.at[pl.ds(idx_ref[base + r], 1), :], buf.at[pl.ds(r, 1), :], sem).start()
        return carry

    lax.fori_loop(0, buf.shape[0], body, 0, unroll=8)


def _wait_row_gather(src_hbm, buf, sem):
    pltpu.make_async_copy(src_hbm.at[pl.ds(0, buf.shape[0]), :], buf, sem).wait()


def _expert_pair_kernel(elo_ref, ehi_ref, nact_ref, tok_ref, v_hbm, wl_ref, wh_ref,
                        g_lo, u_lo, d_lo, g_hi, u_hi, d_hi, o_ref, xbuf, sem):
    del elo_ref, ehi_ref
    i = pl.program_id(0)
    bm = o_ref.shape[0]
    slot = i % 2
    n_active = nact_ref[0]

    @pl.when(i == 0)
    def _():
        _issue_row_gather(tok_ref, 0, v_hbm, xbuf.at[0], sem.at[0])

    @pl.when(i + 1 < n_active)
    def _():
        _issue_row_gather(tok_ref, (i + 1) * bm, v_hbm, xbuf.at[1 - slot], sem.at[1 - slot])

    @pl.when(i >= n_active)
    def _():
        o_ref[...] = jnp.zeros_like(o_ref)

    @pl.when(i < n_active)
    def _():
        _wait_row_gather(v_hbm, xbuf.at[slot], sem.at[slot])
        x = xbuf[slot].astype(BF16)

        def ffn(g_ref, u_ref, d_ref):
            a = jnp.dot(x, g_ref[0], preferred_element_type=F32)
            b = jnp.dot(x, u_ref[0], preferred_element_type=F32)
            act = (a * jax.nn.sigmoid(a) * b).astype(BF16)
            return jnp.dot(act, d_ref[0], preferred_element_type=F32)

        o_ref[...] = ffn(g_lo, u_lo, d_lo) * wl_ref[...] + ffn(g_hi, u_hi, d_hi) * wh_ref[...]


def _expert_pairs(v, row_token, row_wlo, row_whi, blk_lo, blk_hi, n_active, w_gate, w_up, w_down, bm):
    d = v.shape[1]
    n_rows = row_token.shape[0]
    ff = w_gate.shape[2]
    up_spec = lambda which: pl.BlockSpec((1, d, ff), lambda i, lo, hi, na, tok: ((lo, hi)[which][i], 0, 0))
    dn_spec = lambda which: pl.BlockSpec((1, ff, d), lambda i, lo, hi, na, tok: ((lo, hi)[which][i], 0, 0))
    col = pl.BlockSpec((bm, 1), lambda i, lo, hi, na, tok: (i, 0))
    return pl.pallas_call(
        _expert_pair_kernel,
        grid_spec=pltpu.PrefetchScalarGridSpec(
            num_scalar_prefetch=4,
            grid=(n_rows // bm,),
            in_specs=[pl.BlockSpec(memory_space=pl.ANY), col, col,
                      up_spec(0), up_spec(0), dn_spec(0), up_spec(1), up_spec(1), dn_spec(1)],
            out_specs=pl.BlockSpec((bm, d), lambda i, lo, hi, na, tok: (i, 0)),
            scratch_shapes=[pltpu.VMEM((2, bm, d), F32), pltpu.SemaphoreType.DMA((2,))],
        ),
        out_shape=jax.ShapeDtypeStruct((n_rows, d), F32),
        compiler_params=_params("arbitrary"),
        name="expert_pairs",
    )(blk_lo, blk_hi, n_active, row_token, v, row_wlo, row_whi, w_gate, w_up, w_down, w_gate, w_up, w_down)


def _gated_gather_add_kernel(pos_ref, h_ref, y_hbm, g_ref, o_ref, ybuf, sem):
    i = pl.program_id(0)
    tm = o_ref.shape[0]
    slot = i % 2

    @pl.when(i == 0)
    def _():
        _issue_row_gather(pos_ref, 0, y_hbm, ybuf.at[0], sem.at[0])

    @pl.when(i + 1 < pl.num_programs(0))
    def _():
        _issue_row_gather(pos_ref, (i + 1) * tm, y_hbm, ybuf.at[1 - slot], sem.at[1 - slot])

    _wait_row_gather(y_hbm, ybuf.at[slot], sem.at[slot])
    o_ref[...] = h_ref[...] + g_ref[0] * ybuf[slot]


def _gated_gather_add(h, ys, pos, gate, n_rows, tm, n_lat, n_batch):
    d = h.shape[1]
    mod_row = _mod_row_map(tm, n_lat, n_batch)
    return pl.pallas_call(
        _gated_gather_add_kernel,
        grid_spec=pltpu.PrefetchScalarGridSpec(
            num_scalar_prefetch=1,
            grid=(n_rows // tm,),
            in_specs=[pl.BlockSpec((tm, d), lambda i, pos: (i, 0)),
                      pl.BlockSpec(memory_space=pl.ANY),
                      pl.BlockSpec((1, 1, d), lambda i, pos: mod_row(i, 0))],
            out_specs=pl.BlockSpec((tm, d), lambda i, pos: (i, 0)),
            scratch_shapes=[pltpu.VMEM((2, tm, d), F32), pltpu.SemaphoreType.DMA((2,))],
        ),
        out_shape=jax.ShapeDtypeStruct((n_rows, d), F32),
        compiler_params=_params("arbitrary"),
        name="gated_gather_add",
    )(pos, h, ys, gate)


def _dispatch_plan(cls, bm):
    n = cls.shape[0]
    n_blocks = -(-n // bm) + N_CLASSES
    n_rows = n_blocks * bm
    onehot = (cls[:, None] == jnp.arange(N_CLASSES, dtype=jnp.int32)[None, :]).astype(jnp.int32)
    counts = jnp.sum(onehot, axis=0)
    rank = jnp.sum(jnp.cumsum(onehot, axis=0) * onehot, axis=1) - 1
    blocks_per = (counts + bm - 1) // bm
    block_end = jnp.cumsum(blocks_per)
    row_start = (block_end - blocks_per) * bm
    pos = row_start[cls] + rank
    row_token = jnp.zeros((n_rows,), jnp.int32).at[pos].set(jnp.arange(n, dtype=jnp.int32))
    blk_cls = jnp.minimum(jnp.searchsorted(block_end, jnp.arange(n_blocks, dtype=jnp.int32), side="right"),
                          N_CLASSES - 1)
    rows = jnp.arange(n_rows, dtype=jnp.int32)
    row_cls = jnp.repeat(blk_cls, bm)
    row_valid = (rows - row_start[row_cls]) < counts[row_cls]
    return pos, row_token, row_valid, blk_cls, block_end[-1:].astype(jnp.int32)


def _moe_layer(h, scale, shift, gate, w_group, b_group, w_expert, b_expert, w_gate, w_up, w_down,
               n_rows, tm, n_lat, n_batch):
    v, cls8, wt8 = _router(h, scale, shift, w_group, b_group, w_expert, b_expert, n_rows, tm, n_lat, n_batch)
    cls, w_lo, w_hi = cls8[0], wt8[0], wt8[1]
    pos, row_token, row_valid, blk_cls, n_active = _dispatch_plan(cls, MOE_BLOCK)
    row_wlo = jnp.where(row_valid, w_lo[row_token], 0.0)[:, None]
    row_whi = jnp.where(row_valid, w_hi[row_token], 0.0)[:, None]
    blk_lo = jnp.asarray(_CLASS_LO)[blk_cls]
    blk_hi = jnp.asarray(_CLASS_HI)[blk_cls]
    ys = _expert_pairs(v, row_token, row_wlo, row_whi, blk_lo, blk_hi, n_active, w_gate, w_up, w_down, MOE_BLOCK)
    return _gated_gather_add(h, ys, pos, gate, n_rows, min(tm, 512), n_lat, n_batch)


def kernel(x, c, ctx, c_ctx, norm1_w, norm2_w, w_mod, b_mod, even_w_in, even_conv_w, even_w_out,
           odd_w_qkv, odd_q_norm, odd_k_norm, odd_lambda_q1, odd_lambda_k1, odd_lambda_q2, odd_lambda_k2,
           odd_subln_w, odd_w_out, moe_w_group, moe_b_group, moe_w_expert, moe_b_expert,
           moe_w_gate, moe_w_up, moe_w_down):
    n_batch, n_lat, d = x.shape
    n_ctx = ctx.shape[1]
    depth = w_mod.shape[0]
    assert depth == 2 and n_batch < MOD_ROWS and n_lat % n_ctx == 0 and n_lat % GRID_W == 0
    n_lat_rows = n_batch * n_lat
    n_all = n_lat_rows + n_batch * n_ctx
    tm = math.gcd(math.gcd(n_lat, n_batch * n_ctx), 1024)
    tm_seq = math.gcd(n_ctx, 256)

    cond = jnp.zeros((MOD_ROWS, d), F32).at[:n_batch].set(c).at[n_batch].set(c_ctx)
    mod = _modulation(cond, w_mod, b_mod)
    sh1, sc1, g1, sh2, sc2, g2 = [m.reshape(depth, MOD_ROWS, 1, d) for m in jnp.split(mod, 6, axis=-1)]
    scale1 = norm1_w[:, None, None, :] * (1.0 + sc1)
    scale2 = norm2_w[:, None, None, :] * (1.0 + sc2)

    h = jnp.concatenate([x.reshape(n_lat_rows, d), ctx.reshape(n_batch * n_ctx, d)], axis=0)

    p = _ln_matmul(h, scale1[0], sh1[0], even_w_in[0].astype(BF16), n_all, tm, 512, n_lat, n_batch)
    y_conv = _gated_conv(p, even_conv_w[0], n_all, tm_seq, n_lat, n_batch, n_ctx)
    fa, fb = _chan_dft(p, n_all, tm)
    y_four = jnp.concatenate([_seq_dft_lat(fa, fb, n_lat, n_batch, min(512, n_lat), 512),
                              _seq_dft_ctx(fa, fb, n_lat, n_batch, n_ctx)], axis=0)
    h = _proj_residual(y_conv, y_four, 0, 0, even_w_out[0].astype(BF16), h, g1[0], n_all, tm, 1024,
                       n_lat, n_batch)
    h = _moe_layer(h, scale2[0], sh2[0], g2[0], moe_w_group[0], moe_b_group[0], moe_w_expert[0],
                   moe_b_expert[0], moe_w_gate[0].astype(BF16), moe_w_up[0].astype(BF16),
                   moe_w_down[0].astype(BF16), n_all, tm, n_lat, n_batch)

    lam_init = 0.8 - 0.6 * math.exp(-0.3 * 1)
    lam = (jnp.exp(jnp.sum(odd_lambda_q1[0].astype(F32) * odd_lambda_k1[0].astype(F32)))
           - jnp.exp(jnp.sum(odd_lambda_q2[0].astype(F32) * odd_lambda_k2[0].astype(F32))) + lam_init)
    tab = _rope_tables(n_lat, tm, odd_q_norm[0] * (DIFF_HEAD_DIM ** -0.5 * math.log2(math.e)), odd_k_norm[0])
    qkv = _qkv_proj(h, scale1[1], sh1[1], odd_w_qkv[0].astype(BF16), tab, n_all, tm, 512, n_lat, n_batch)
    subln_gain = (odd_subln_w[0].astype(F32) * (1.0 - lam_init)).reshape(1, DIFF_V_DIM)
    o = _diff_attention(qkv, lam.reshape(1).astype(F32), subln_gain, n_lat, n_batch, n_ctx,
                        min(256, n_lat), min(512, n_lat))
    h = _proj_residual(o, o, 0, 1, odd_w_out[0].astype(BF16), h, g1[1], n_lat_rows, tm, 1024, n_lat, n_batch)
    h = _moe_layer(h, scale2[1], sh2[1], g2[1], moe_w_group[1], moe_b_group[1], moe_w_expert[1],
                   moe_b_expert[1], moe_w_gate[1].astype(BF16), moe_w_up[1].astype(BF16),
                   moe_w_down[1].astype(BF16), n_lat_rows, tm, n_lat, n_batch)
    return h.reshape(n_batch, n_lat, d)
```
